```python
import math
import jax, jax.numpy as jnp
from jax import lax
import numpy as np

D_MODEL = 1024
BATCH = 8
SEQ = 4096
DEPTH = 4
DEC_BATCH = 1
DEC_SEQ = 16384
PAST_LEN = 128

GRID_W = 64
HEAD_DIM = 64
N_HEADS_A = 8
NA_ROWS = 8
NA_COLS = 16
N_HEADS_B = 8
N_KV_B = 2
GROUP_B = N_HEADS_B // N_KV_B
ROPE_THETA = 10000.0
C_WIDTH = 512
CONV_K = 31
N_HEADS_D = 4
D_V_DIFF = 2 * HEAD_DIM
Q_BLOCK = 128
D_FF = -(-8 * D_MODEL // (3 * 256)) * 256
EPS = 1e-6
LN_EPS = 1e-5
N_AB = (DEPTH + 1) // 2
N_CD = DEPTH // 2
W_A = N_HEADS_A * HEAD_DIM
W_BQ = N_HEADS_B * HEAD_DIM
W_BKV = N_KV_B * HEAD_DIM
W_DQ = N_HEADS_D * 2 * HEAD_DIM
W_DV = N_HEADS_D * D_V_DIFF
AB_SPLITS = (W_A, W_A, W_A, W_BQ, W_BKV, W_BKV)
CD_SPLITS = (C_WIDTH, C_WIDTH, W_DQ, W_DQ, W_DV)
AB_IN = sum(AB_SPLITS)
CD_IN = sum(CD_SPLITS)
AB_OUT = W_A + W_BQ
CD_OUT = C_WIDTH + W_DV
ALIBI_SLOPES = tuple(2.0 ** (-8.0 * (h + 1) / N_HEADS_D) for h in range(N_HEADS_D))

kernel_name = 'hybrid_natten_gqa_conformer_diffattn_encoder'

F32 = jnp.float32


def _split(z, sizes):
    offs = np.cumsum(sizes)[:-1].tolist()
    return jnp.split(z, offs, axis=-1)


def _rms_norm(x, g):
    xf = x.astype(F32)
    y = xf * lax.rsqrt(jnp.mean(xf * xf, axis=-1, keepdims=True) + EPS)
    return (y * g.astype(F32)).astype(x.dtype)


def _layer_norm(x, g, b):
    xf = x.astype(F32)
    mu = jnp.mean(xf, axis=-1, keepdims=True)
    var = jnp.mean(jnp.square(xf - mu), axis=-1, keepdims=True)
    y = (xf - mu) * lax.rsqrt(var + LN_EPS)
    return (y * g.astype(F32) + b.astype(F32)).astype(x.dtype)


def _axial_rope(x):
    L = x.shape[1]
    t = jnp.arange(L, dtype=jnp.int32)
    row = (t // GRID_W).astype(F32)
    col = (t % GRID_W).astype(F32)
    half = HEAD_DIM // 2
    inv = ROPE_THETA ** (-jnp.arange(0, half, 2, dtype=F32) / half)
    ang = jnp.concatenate([row[:, None] * inv, col[:, None] * inv], axis=-1)
    cos = jnp.cos(ang)[None, :, None, :].astype(x.dtype)
    sin = jnp.sin(ang)[None, :, None, :].astype(x.dtype)
    x1 = x[..., 0::2]
    x2 = x[..., 1::2]
    out = jnp.stack([x1 * cos - x2 * sin, x1 * sin + x2 * cos], axis=-1)
    return out.reshape(x.shape)


def _neighborhood_attention(q, k, v, rpb):
    B, L, H, dh = q.shape
    rows = L // GRID_W
    kh = min(NA_ROWS, rows)
    n_nb = kh * NA_COLS
    scale = dh ** -0.5
    col = np.arange(GRID_W)
    cs = np.clip(col - NA_COLS // 2, 0, GRID_W - NA_COLS)
    kcol = cs[:, None] + np.arange(NA_COLS)[None, :]
    dcol = kcol - col[:, None]
    qb = q.reshape(B, rows, GRID_W, H, dh).swapaxes(0, 1)

    def row_block(args):
        qblk, r = args
        rs = jnp.clip(r - kh // 2, 0, rows - kh)
        krow = rs + jnp.arange(kh, dtype=jnp.int32)
        idx = (krow[None, :, None] * GRID_W + kcol[:, None, :]).reshape(GRID_W, n_nb)
        kn = k[:, idx]
        vn = v[:, idx]
        s = jnp.einsum('bqhd,bqnhd->bhqn', qblk, kn).astype(F32) * scale
        drow = krow - r
        bias = rpb[:, drow[None, :, None] + (NA_ROWS - 1), dcol[:, None, :] + (NA_COLS - 1)]
        s = s + bias.reshape(H, GRID_W, n_nb).astype(F32)[None]
        p = jax.nn.softmax(s, axis=-1).astype(v.dtype)
        return jnp.einsum('bhqn,bqnhd->bqhd', p, vn)

    o = lax.map(row_block, (qb, jnp.arange(rows, dtype=jnp.int32)))
    return o.swapaxes(0, 1).reshape(B, L, H * dh)


def _gqa_attention(q, k, v):
    B, L = q.shape[:2]
    nb = L // Q_BLOCK
    scale = HEAD_DIM ** -0.5
    qb = q.reshape(B, nb, Q_BLOCK, N_KV_B, GROUP_B, HEAD_DIM).swapaxes(0, 1)

    def block(qblk):
        s = jnp.einsum('bqkgd,bskd->bkgqs', qblk, k).astype(F32) * scale
        p = jax.nn.softmax(s, axis=-1).astype(v.dtype)
        return jnp.einsum('bkgqs,bskd->bqkgd', p, v)

    o = lax.map(block, qb)
    return o.swapaxes(0, 1).reshape(B, L, N_HEADS_B * HEAD_DIM)


def _diff_attention(q, k, v, lam):
    B, L = q.shape[:2]
    nb = L // Q_BLOCK
    scale = HEAD_DIM ** -0.5
    slopes = jnp.asarray(ALIBI_SLOPES, F32)
    kpos = jnp.arange(L, dtype=jnp.int32)
    qb = q.reshape(B, nb, Q_BLOCK, 2, N_HEADS_D, HEAD_DIM).swapaxes(0, 1)
    qpos = kpos.reshape(nb, Q_BLOCK)

    def block(args):
        qblk, qp = args
        s = jnp.einsum('bqmhd,bsmhd->bmhqs', qblk, k).astype(F32) * scale
        dist = jnp.abs(qp[:, None] - kpos[None, :]).astype(F32)
        s = s - slopes[:, None, None] * dist
        p = jax.nn.softmax(s, axis=-1)
        a = (p[:, 0] - lam * p[:, 1]).astype(v.dtype)
        return jnp.einsum('bhqs,bshe->bqhe', a, v)

    o = lax.map(block, (qb, qpos))
    return o.swapaxes(0, 1).reshape(B, L, N_HEADS_D, D_V_DIFF)


def _mixer_ab(h, p, j):
    B, L, _ = h.shape
    z = h @ p['w_in_ab'][j]
    qa, ka, va, qb, kb, vb = _split(z, AB_SPLITS)
    shp_a = (B, L, N_HEADS_A, HEAD_DIM)
    oa = _neighborhood_attention(qa.reshape(shp_a), ka.reshape(shp_a), va.reshape(shp_a), p['rpb_a'][j])
    qb = _axial_rope(_rms_norm(qb.reshape(B, L, N_HEADS_B, HEAD_DIM), p['qnorm_b'][j]))
    kb = _axial_rope(_rms_norm(kb.reshape(B, L, N_KV_B, HEAD_DIM), p['knorm_b'][j]))
    vb = vb.reshape(B, L, N_KV_B, HEAD_DIM)
    ob = _gqa_attention(qb, kb, vb)
    return jnp.concatenate([oa, ob], axis=-1) @ p['w_out_ab'][j]


def _mixer_cd(h, p, j, layer_idx):
    B, L, _ = h.shape
    z = h @ p['w_in_cd'][j]
    ca, cg, q, k, v = _split(z, CD_SPLITS)
    u = ca * jax.nn.sigmoid(cg)
    u = lax.conv_general_dilated(u, p['conv_w_c'][j][:, None, :], window_strides=(1,),
                                 padding=[(CONV_K // 2, CONV_K // 2)],
                                 dimension_numbers=('NWC', 'WIO', 'NWC'),
                                 feature_group_count=C_WIDTH)
    u = u + p['conv_b_c'][j]
    u = jax.nn.silu(_layer_norm(u, p['conv_ln_g'][j], p['conv_ln_b'][j]))
    lam_init = 0.8 - 0.6 * math.exp(-0.3 * layer_idx)
    lam = (jnp.exp(jnp.sum(p['lam_q1'][j].astype(F32) * p['lam_k1'][j].astype(F32)))
           - jnp.exp(jnp.sum(p['lam_q2'][j].astype(F32) * p['lam_k2'][j].astype(F32))) + lam_init)
    q = q.reshape(B, L, N_HEADS_D, 2, HEAD_DIM).swapaxes(2, 3)
    k = k.reshape(B, L, N_HEADS_D, 2, HEAD_DIM).swapaxes(2, 3)
    v = v.reshape(B, L, N_HEADS_D, D_V_DIFF)
    od = _diff_attention(q, k, v, lam)
    od = (_rms_norm(od, p['subln_g'][j]) * (1.0 - lam_init)).reshape(B, L, W_DV)
    return jnp.concatenate([u, od], axis=-1) @ p['w_out_cd'][j]


def _trunk(x, c, p):
    cs = jax.nn.silu(c)
    for li in range(DEPTH):
        mod = cs @ p['w_mod'][li] + p['b_mod'][li]
        sh1, sc1, g1, sh2, sc2, g2 = [m[:, None, :] for m in jnp.split(mod, 6, axis=-1)]
        h = _rms_norm(x, p['norm_mix_g'][li]) * (1.0 + sc1) + sh1
        if li % 2 == 0:
            out = _mixer_ab(h, p, li // 2)
        else:
            out = _mixer_cd(h, p, li // 2, li)
        x = x + (1.0 + g1) * out
        h = _rms_norm(x, p['norm_ffn_g'][li]) * (1.0 + sc2) + sh2
        ff = (jax.nn.silu(h @ p['w1'][li]) * (h @ p['w3'][li])) @ p['w2'][li]
        x = x + (1.0 + g2) * ff
    return _rms_norm(x, p['final_g'])


def setup_inputs(seed: int = 0) -> dict:
    key = jax.random.key(seed)
    ks = jax.random.split(key, 32)

    def nrm(k, shape, s):
        return jax.random.normal(k, shape, F32) * s

    D = D_MODEL
    return {
        'x_prompt': nrm(ks[0], (BATCH, SEQ, D), 1.0),
        'x_sample': nrm(ks[1], (DEC_BATCH, DEC_SEQ, D), 1.0),
        'c_prompt': nrm(ks[2], (BATCH, D), 1.0),
        'c_sample': nrm(ks[3], (DEC_BATCH, D), 1.0),
        'w_mod': nrm(ks[4], (DEPTH, D, 6 * D), 0.1 * D ** -0.5),
        'b_mod': nrm(ks[5], (DEPTH, 6 * D), 0.02),
        'norm_mix_g': 1.0 + nrm(ks[6], (DEPTH, D), 0.05),
        'norm_ffn_g': 1.0 + nrm(ks[7], (DEPTH, D), 0.05),
        'w_in_ab': nrm(ks[8], (N_AB, D, AB_IN), D ** -0.5),
        'rpb_a': nrm(ks[9], (N_AB, N_HEADS_A, 2 * NA_ROWS - 1, 2 * NA_COLS - 1), 0.1),
        'qnorm_b': 1.0 + nrm(ks[10], (N_AB, HEAD_DIM), 0.05),
        'knorm_b': 1.0 + nrm(ks[11], (N_AB, HEAD_DIM), 0.05),
        'w_out_ab': nrm(ks[12], (N_AB, AB_OUT, D), AB_OUT ** -0.5),
        'w_in_cd': nrm(ks[13], (N_CD, D, CD_IN), D ** -0.5),
        'conv_w_c': nrm(ks[14], (N_CD, CONV_K, C_WIDTH), CONV_K ** -0.5),
        'conv_b_c': nrm(ks[15], (N_CD, C_WIDTH), 0.02),
        'conv_ln_g': 1.0 + nrm(ks[16], (N_CD, C_WIDTH), 0.05),
        'conv_ln_b': nrm(ks[17], (N_CD, C_WIDTH), 0.02),
        'lam_q1': nrm(ks[18], (N_CD, HEAD_DIM), 0.1),
        'lam_k1': nrm(ks[19], (N_CD, HEAD_DIM), 0.1),
        'lam_q2': nrm(ks[20], (N_CD, HEAD_DIM), 0.1),
        'lam_k2': nrm(ks[21], (N_CD, HEAD_DIM), 0.1),
        'subln_g': 1.0 + nrm(ks[22], (N_CD, D_V_DIFF), 0.05),
        'w_out_cd': nrm(ks[23], (N_CD, CD_OUT, D), CD_OUT ** -0.5),
        'w1': nrm(ks[24], (DEPTH, D, D_FF), D ** -0.5),
        'w3': nrm(ks[25], (DEPTH, D, D_FF), D ** -0.5),
        'w2': nrm(ks[26], (DEPTH, D_FF, D), D_FF ** -0.5),
        'final_g': 1.0 + nrm(ks[27], (D,), 0.05),
    }


def reference(x_prompt, x_sample, c_prompt, c_sample, w_mod, b_mod, norm_mix_g, norm_ffn_g,
              w_in_ab, rpb_a, qnorm_b, knorm_b, w_out_ab, w_in_cd, conv_w_c, conv_b_c,
              conv_ln_g, conv_ln_b, lam_q1, lam_k1, lam_q2, lam_k2, subln_g, w_out_cd,
              w1, w3, w2, final_g):
    params = dict(w_mod=w_mod, b_mod=b_mod, norm_mix_g=norm_mix_g, norm_ffn_g=norm_ffn_g,
                  w_in_ab=w_in_ab, rpb_a=rpb_a, qnorm_b=qnorm_b, knorm_b=knorm_b, w_out_ab=w_out_ab,
                  w_in_cd=w_in_cd, conv_w_c=conv_w_c, conv_b_c=conv_b_c, conv_ln_g=conv_ln_g,
                  conv_ln_b=conv_ln_b, lam_q1=lam_q1, lam_k1=lam_k1, lam_q2=lam_q2, lam_k2=lam_k2,
                  subln_g=subln_g, w_out_cd=w_out_cd, w1=w1, w3=w3, w2=w2, final_g=final_g)
    y_prompt = _trunk(x_prompt, c_prompt, params)
    y_sample = _trunk(x_sample, c_sample, params)
    return (y_prompt, y_sample)
```

```python
import functools
import math

import numpy as np
import jax
import jax.numpy as jnp
from jax import lax
from jax.experimental import pallas as pl
from jax.experimental.pallas import tpu as pltpu

F32 = jnp.float32
BF16 = jnp.bfloat16

D_MODEL = 1024
DEPTH = 4
GRID_W = 64
HEAD_DIM = 64
N_HEADS_A = 8
NA_ROWS = 8
NA_COLS = 16
N_HEADS_B = 8
N_KV_B = 2
GROUP_B = N_HEADS_B // N_KV_B
ROPE_THETA = 10000.0
C_WIDTH = 512
CONV_K = 31
N_HEADS_D = 4
D_V_DIFF = 2 * HEAD_DIM
D_FF = 2816
EPS = 1e-6
LN_EPS = 1e-5
ALIBI_SLOPES = tuple(2.0 ** (-8.0 * (h + 1) / N_HEADS_D) for h in range(N_HEADS_D))
LOG2E = math.log2(math.e)
SCALE = HEAD_DIM ** -0.5
NEG_BIG = -1e30

LANES = 128
VMEM_LIMIT = 56 * 1024 * 1024

TM = 512
TM_FFN = 1024
TN_FFN = 1408
TK = 512
TQ_GQA = 256
TQ_DIFF = 256
NA_R = 8
CONV_T = 256
CONV_HALO = 16
CONV_CHUNK = 64


def _cparams(sem):
    return pltpu.CompilerParams(dimension_semantics=sem, vmem_limit_bytes=VMEM_LIMIT)


def _dot(a, b):
    return jnp.dot(a, b, preferred_element_type=F32)


def _dot_nt(a, b):
    return lax.dot_general(a, b, (((1,), (1,)), ((), ())), preferred_element_type=F32)


def _silu(x):
    return x * jax.nn.sigmoid(x)


def _rms_mod(x, g, sc, sh):
    ms = jnp.mean(x * x, axis=-1, keepdims=True)
    h = x * lax.rsqrt(ms + EPS) * g
    return h * (1.0 + sc) + sh


def _mod_kernel(c_ref, w_ref, b_ref, o_ref):
    cs = _silu(c_ref[...])
    o_ref[...] = _dot(cs.astype(BF16), w_ref[...].astype(BF16)) + b_ref[...]


def _modulation(c_all, w_mod, b_mod):
    R = c_all.shape[0]
    tn = 1536
    return pl.pallas_call(
        _mod_kernel,
        grid=(DEPTH, 6 * D_MODEL // tn),
        in_specs=[
            pl.BlockSpec((R, D_MODEL), lambda l, j: (0, 0)),
            pl.BlockSpec((None, D_MODEL, tn), lambda l, j: (l, 0, j)),
            pl.BlockSpec((None, 1, tn), lambda l, j: (l, 0, j)),
        ],
        out_specs=pl.BlockSpec((None, R, tn), lambda l, j: (l, 0, j)),
        out_shape=jax.ShapeDtypeStruct((DEPTH, R, 6 * D_MODEL), F32),
        compiler_params=_cparams(("arbitrary", "arbitrary")),
        name="modulation",
    )(c_all, w_mod, b_mod.reshape(DEPTH, 1, 6 * D_MODEL))


def _inproj_kernel(x_ref, g_ref, sc_ref, sh_ref, w_ref, *o_refs, splits):
    h = _rms_mod(x_ref[...], g_ref[...], sc_ref[...], sh_ref[...])
    z = _dot(h.astype(BF16), w_ref[...])
    off = 0
    for o_ref, n in zip(o_refs, splits):
        o_ref[...] = z[:, off:off + n].astype(o_ref.dtype)
        off += n


def _inproj(x, g, sc, sh, w, L, splits, dtypes):
    T = x.shape[0]
    N = w.shape[1]
    per_b = L // TM
    vec = pl.BlockSpec((None, 1, D_MODEL), lambda i: (i // per_b, 0, 0))
    return pl.pallas_call(
        functools.partial(_inproj_kernel, splits=splits),
        grid=(T // TM,),
        in_specs=[
            pl.BlockSpec((TM, D_MODEL), lambda i: (i, 0)),
            pl.BlockSpec((1, D_MODEL), lambda i: (0, 0)),
            vec, vec,
            pl.BlockSpec((D_MODEL, N), lambda i: (0, 0)),
        ],
        out_specs=[pl.BlockSpec((TM, n), lambda i: (i, 0)) for n in splits],
        out_shape=[jax.ShapeDtypeStruct((T, n), dt) for n, dt in zip(splits, dtypes)],
        compiler_params=_cparams(("parallel",)),
        name="inproj",
    )(x, g.reshape(1, D_MODEL), sc, sh, w)


def _outproj_kernel(x_ref, a_ref, b_ref, gate_ref, w_ref, o_ref):
    half = a_ref.shape[1]
    y = _dot(a_ref[...], w_ref[:half, :]) + _dot(b_ref[...], w_ref[half:, :])
    o_ref[...] = x_ref[...] + (1.0 + gate_ref[...]) * y


def _outproj(x, a, b, gate, w, L):
    T = x.shape[0]
    per_b = L // TM
    half = a.shape[1]
    return pl.pallas_call(
        _outproj_kernel,
        grid=(T // TM,),
        in_specs=[
            pl.BlockSpec((TM, D_MODEL), lambda i: (i, 0)),
            pl.BlockSpec((TM, half), lambda i: (i, 0)),
            pl.BlockSpec((TM, half), lambda i: (i, 0)),
            pl.BlockSpec((None, 1, D_MODEL), lambda i: (i // per_b, 0, 0)),
            pl.BlockSpec((2 * half, D_MODEL), lambda i: (0, 0)),
        ],
        out_specs=pl.BlockSpec((TM, D_MODEL), lambda i: (i, 0)),
        out_shape=jax.ShapeDtypeStruct((T, D_MODEL), F32),
        compiler_params=_cparams(("parallel",)),
        name="outproj",
    )(x, a, b, gate, w)


def _ffn_kernel(x_ref, g_ref, sc_ref, sh_ref, gate_ref, w1_ref, w3_ref, w2_ref, fg_ref, o_ref,
                h_scr, acc_scr, *, final):
    j = pl.program_id(1)

    @pl.when(j == 0)
    def _():
        h_scr[...] = _rms_mod(x_ref[...], g_ref[...], sc_ref[...], sh_ref[...]).astype(BF16)
        acc_scr[...] = jnp.zeros_like(acc_scr)

    h = h_scr[...]
    t = _silu(_dot(h, w1_ref[...])) * _dot(h, w3_ref[...])
    acc_scr[...] += _dot(t.astype(BF16), w2_ref[...])

    @pl.when(j == pl.num_programs(1) - 1)
    def _():
        y = x_ref[...] + (1.0 + gate_ref[...]) * acc_scr[...]
        if final:
            ms = jnp.mean(y * y, axis=-1, keepdims=True)
            y = y * lax.rsqrt(ms + EPS) * fg_ref[...]
        o_ref[...] = y


def _ffn(x, g, sc, sh, gate, w1, w3, w2, final_g, L, final):
    T = x.shape[0]
    tm = min(TM_FFN, L)
    per_b = L // tm
    vec = pl.BlockSpec((None, 1, D_MODEL), lambda i, j: (i // per_b, 0, 0))
    row = pl.BlockSpec((1, D_MODEL), lambda i, j: (0, 0))
    return pl.pallas_call(
        functools.partial(_ffn_kernel, final=final),
        grid=(T // tm, D_FF // TN_FFN),
        in_specs=[
            pl.BlockSpec((tm, D_MODEL), lambda i, j: (i, 0)),
            row, vec, vec, vec,
            pl.BlockSpec((D_MODEL, TN_FFN), lambda i, j: (0, j)),
            pl.BlockSpec((D_MODEL, TN_FFN), lambda i, j: (0, j)),
            pl.BlockSpec((TN_FFN, D_MODEL), lambda i, j: (j, 0)),
            row,
        ],
        out_specs=pl.BlockSpec((tm, D_MODEL), lambda i, j: (i, 0)),
        out_shape=jax.ShapeDtypeStruct((T, D_MODEL), F32),
        scratch_shapes=[pltpu.VMEM((tm, D_MODEL), BF16), pltpu.VMEM((tm, D_MODEL), F32)],
        compiler_params=_cparams(("parallel", "arbitrary")),
        name="ffn",
    )(x, g.reshape(1, D_MODEL), sc, sh, gate, w1, w3, w2, final_g.reshape(1, D_MODEL))


def _softmax_step(s, pv_fn, m_ref, l_ref, acc_ref):
    m_prev = m_ref[...]
    m_new = jnp.maximum(m_prev, jnp.max(s, axis=0, keepdims=True))
    alpha = jnp.exp2(m_prev - m_new)
    p = jnp.exp2(s - m_new)
    l_ref[...] = alpha * l_ref[...] + jnp.sum(p, axis=0, keepdims=True)
    acc_ref[...] = alpha * acc_ref[...] + pv_fn(p.astype(BF16))
    m_ref[...] = m_new


def _softmax_init(m_ref, l_ref, acc_ref):
    m_ref[...] = jnp.full_like(m_ref, NEG_BIG)
    l_ref[...] = jnp.zeros_like(l_ref)
    acc_ref[...] = jnp.zeros_like(acc_ref)


def _seg_mean_sq(x, seg_ref):
    sq = x * x
    hi = sq.astype(BF16)
    lo = (sq - hi.astype(F32)).astype(BF16)
    n = x.shape[1]
    seg = seg_ref[:n, :n]
    return (_dot(hi, seg) + _dot(lo, seg)) * (1.0 / HEAD_DIM)


def _rope(x, cos, sin_signed, first_half):
    n = x.shape[1]
    half = HEAD_DIM // 2
    rot = jnp.where(first_half, pltpu.roll(x, n - half, 1), pltpu.roll(x, half, 1))
    return x * cos + rot * sin_signed


def _gqa_prep_kernel(z_ref, cos_ref, sin_ref, gq_ref, gk_ref, seg_ref, q_ref, k_ref, vt_ref):
    wq = N_HEADS_B * HEAD_DIM
    wk = N_KV_B * HEAD_DIM
    z = z_ref[...]
    cos2 = cos_ref[...]
    sin2 = sin_ref[...]
    tm = z.shape[0]

    def first_half(width):
        lane = lax.broadcasted_iota(jnp.int32, (tm, width), 1)
        return (lane % HEAD_DIM) < (HEAD_DIM // 2)

    xq = z[:, :wq]
    xq = xq * lax.rsqrt(_seg_mean_sq(xq, seg_ref) + EPS) * gq_ref[...]
    cosq = jnp.concatenate([cos2] * (wq // LANES), axis=1)
    sinq = jnp.concatenate([sin2] * (wq // LANES), axis=1)
    q = _rope(xq, cosq, sinq, first_half(wq))
    for h in range(N_HEADS_B):
        q_ref[h] = q[:, h * HEAD_DIM:(h + 1) * HEAD_DIM].astype(BF16)

    xk = z[:, wq:wq + wk]
    xk = xk * lax.rsqrt(_seg_mean_sq(xk, seg_ref) + EPS) * gk_ref[...]
    k = _rope(xk, cos2, sin2, first_half(wk))
    for g in range(N_KV_B):
        k_ref[g] = k[:, g * HEAD_DIM:(g + 1) * HEAD_DIM].astype(BF16)

    vt = z[:, wq + wk:].T
    for g in range(N_KV_B):
        vt_ref[g] = vt[g * HEAD_DIM:(g + 1) * HEAD_DIM, :].astype(BF16)


def _gqa_prep(zb, cos2, sin2, gq, gk, seg, L):
    T = zb.shape[0]
    per_b = L // TK
    wq = N_HEADS_B * HEAD_DIM
    wk = N_KV_B * HEAD_DIM
    return pl.pallas_call(
        _gqa_prep_kernel,
        grid=(T // TK,),
        in_specs=[
            pl.BlockSpec((TK, wq + 2 * wk), lambda i: (i, 0)),
            pl.BlockSpec((TK, LANES), lambda i: (i % per_b, 0)),
            pl.BlockSpec((TK, LANES), lambda i: (i % per_b, 0)),
            pl.BlockSpec((1, wq), lambda i: (0, 0)),
            pl.BlockSpec((1, wk), lambda i: (0, 0)),
            pl.BlockSpec((wq, wq), lambda i: (0, 0)),
        ],
        out_specs=[
            pl.BlockSpec((N_HEADS_B, TK, HEAD_DIM), lambda i: (0, i, 0)),
            pl.BlockSpec((N_KV_B, TK, HEAD_DIM), lambda i: (0, i, 0)),
            pl.BlockSpec((N_KV_B, None, HEAD_DIM, TK), lambda i: (0, i, 0, 0)),
        ],
        out_shape=[
            jax.ShapeDtypeStruct((N_HEADS_B, T, HEAD_DIM), BF16),
            jax.ShapeDtypeStruct((N_KV_B, T, HEAD_DIM), BF16),
            jax.ShapeDtypeStruct((N_KV_B, T // TK, HEAD_DIM, TK), BF16),
        ],
        compiler_params=_cparams(("parallel",)),
        name="gqa_prep",
    )(zb, cos2, sin2, gq, gk, seg)


def _gqa_attn_kernel(q_ref, k_ref, vt_ref, o_ref, m_ref, l_ref, acc_ref):
    tq = q_ref.shape[1]
    nkv = vt_ref.shape[0]
    qs = q_ref[...].reshape(GROUP_B * tq, HEAD_DIM)
    _softmax_init(m_ref, l_ref, acc_ref)

    def body(j, carry):
        kt = k_ref[pl.ds(pl.multiple_of(j * TK, TK), TK), :]
        s = _dot_nt(kt, qs)
        _softmax_step(s, lambda p: _dot(vt_ref[j], p), m_ref, l_ref, acc_ref)
        return carry

    lax.fori_loop(0, nkv, body, 0)
    o = acc_ref[...] / l_ref[...]
    for pair in range(GROUP_B // 2):
        two = jnp.concatenate([o[:, (2 * pair) * tq:(2 * pair + 1) * tq],
                               o[:, (2 * pair + 1) * tq:(2 * pair + 2) * tq]], axis=0)
        o_ref[:, pair * LANES:(pair + 1) * LANES] = two.T.astype(o_ref.dtype)


def _gqa_attn(q, k, vt, B, L):
    T = B * L
    tq = TQ_GQA
    nq = L // tq
    nkv = L // TK
    nqs = GROUP_B * tq
    return pl.pallas_call(
        _gqa_attn_kernel,
        grid=(B, N_KV_B, nq),
        in_specs=[
            pl.BlockSpec((GROUP_B, tq, HEAD_DIM), lambda b, g, i: (g, b * nq + i, 0)),
            pl.BlockSpec((None, L, HEAD_DIM), lambda b, g, i: (g, b, 0)),
            pl.BlockSpec((None, nkv, HEAD_DIM, TK), lambda b, g, i: (g, b, 0, 0)),
        ],
        out_specs=pl.BlockSpec((tq, GROUP_B * HEAD_DIM), lambda b, g, i: (b * nq + i, g)),
        out_shape=jax.ShapeDtypeStruct((T, N_HEADS_B * HEAD_DIM), BF16),
        scratch_shapes=[pltpu.VMEM((1, nqs), F32), pltpu.VMEM((1, nqs), F32),
                        pltpu.VMEM((HEAD_DIM, nqs), F32)],
        compiler_params=_cparams(("parallel", "parallel", "arbitrary")),
        name="gqa_attn",
    )(q, k, vt)


def _na_kernel(q_ref, k_ref, v_ref, bias_ref, o_ref, *, rows):
    i = pl.program_id(2)
    lane = lax.broadcasted_iota(jnp.int32, (GRID_W, LANES), 1)
    nwin = NA_ROWS * GRID_W

    def row(rr, carry):
        r = i * NA_R + rr
        rs = jnp.clip(r - NA_ROWS // 2, 0, rows - NA_ROWS)
        delta = r - rs
        q = q_ref[pl.ds(pl.multiple_of(rr * GRID_W, GRID_W), GRID_W), :]
        kw = k_ref[pl.ds(pl.multiple_of(rs * GRID_W, GRID_W), nwin), :]
        vw = v_ref[pl.ds(pl.multiple_of(rs * GRID_W, GRID_W), nwin), :]
        outs = []
        for hh in range(2):
            in_head = (lane >= hh * HEAD_DIM) & (lane < (hh + 1) * HEAD_DIM)
            qm = jnp.where(in_head, q, jnp.zeros_like(q))
            s = _dot_nt(qm, kw) + bias_ref[hh, delta]
            m = jnp.max(s, axis=-1, keepdims=True)
            p = jnp.exp2(s - m)
            l = jnp.sum(p, axis=-1, keepdims=True)
            outs.append(_dot(p.astype(BF16), vw) / l)
        o = jnp.where(lane < HEAD_DIM, outs[0], outs[1])
        o_ref[pl.ds(pl.multiple_of(rr * GRID_W, GRID_W), GRID_W), :] = o.astype(o_ref.dtype)
        return carry

    lax.fori_loop(0, NA_R, row, 0)


def _na_attn(za, bias, B, L):
    T = B * L
    rows = L // GRID_W
    assert rows >= NA_ROWS and rows % NA_R == 0
    nr = rows // NA_R
    blk = NA_R * GRID_W
    npair = N_HEADS_A // 2
    return pl.pallas_call(
        functools.partial(_na_kernel, rows=rows),
        grid=(B, npair, nr),
        in_specs=[
            pl.BlockSpec((blk, LANES), lambda b, hp, i: (b * nr + i, hp)),
            pl.BlockSpec((L, LANES), lambda b, hp, i: (b, npair + hp)),
            pl.BlockSpec((L, LANES), lambda b, hp, i: (b, 2 * npair + hp)),
            pl.BlockSpec((2, NA_ROWS, GRID_W, NA_ROWS * GRID_W), lambda b, hp, i: (hp, 0, 0, 0)),
        ],
        out_specs=pl.BlockSpec((blk, LANES), lambda b, hp, i: (b * nr + i, hp)),
        out_shape=jax.ShapeDtypeStruct((T, N_HEADS_A * HEAD_DIM), BF16),
        compiler_params=_cparams(("parallel", "parallel", "arbitrary")),
        name="na_attn",
    )(za, za, za, bias)


def _na_bias_table(rpb):
    col = np.arange(GRID_W)
    cs = np.clip(col - NA_COLS // 2, 0, GRID_W - NA_COLS)
    kc = np.arange(GRID_W)
    valid = (kc[None, :] >= cs[:, None]) & (kc[None, :] < cs[:, None] + NA_COLS)
    dcol = np.clip(kc[None, :] - col[:, None] + (NA_COLS - 1), 0, 2 * NA_COLS - 2)
    delta = np.arange(NA_ROWS)
    j = np.arange(NA_ROWS)
    drow = j[None, :] - delta[:, None] + (NA_ROWS - 1)
    t = rpb[:, drow[:, None, :, None], dcol[None, :, None, :]]
    t = jnp.where(valid[None, None, :, None, :], t * LOG2E, NEG_BIG)
    return t.reshape(N_HEADS_A, NA_ROWS, GRID_W, NA_ROWS * GRID_W).astype(F32)


def _conv_kernel(z_ref, zp_ref, zn_ref, w_ref, b_ref, lg_ref, lb_ref, o_ref, u_scr):
    i = pl.program_id(1)
    nt = pl.num_programs(1)
    T = z_ref.shape[0]

    def glu(z):
        return z[:, :C_WIDTH] * jax.nn.sigmoid(z[:, C_WIDTH:])

    u_scr[CONV_HALO:CONV_HALO + T, :] = glu(z_ref[...])
    u_scr[:CONV_HALO, :] = jnp.where(i == 0, 0.0, glu(zp_ref[...]))
    u_scr[CONV_HALO + T:, :] = jnp.where(i == nt - 1, 0.0, glu(zn_ref[...]))

    off = CONV_HALO - CONV_K // 2
    for c in range(T // CONV_CHUNK):
        acc = jnp.zeros((CONV_CHUNK, C_WIDTH), F32)
        for k in range(CONV_K):
            acc = acc + w_ref[k:k + 1, :] * u_scr[pl.ds(c * CONV_CHUNK + off + k, CONV_CHUNK), :]
        y = acc + b_ref[...]
        mu = jnp.mean(y, axis=-1, keepdims=True)
        yc = y - mu
        var = jnp.mean(yc * yc, axis=-1, keepdims=True)
        y = yc * lax.rsqrt(var + LN_EPS) * lg_ref[...] + lb_ref[...]
        o_ref[c * CONV_CHUNK:(c + 1) * CONV_CHUNK, :] = _silu(y).astype(o_ref.dtype)


def _conv_branch(zc, w, b, lg, lb, B, L):
    T = B * L
    nt = L // CONV_T
    hb = CONV_T // CONV_HALO
    last = T // CONV_HALO - 1
    row = pl.BlockSpec((1, C_WIDTH), lambda bb, i: (0, 0))
    return pl.pallas_call(
        _conv_kernel,
        grid=(B, nt),
        in_specs=[
            pl.BlockSpec((CONV_T, 2 * C_WIDTH), lambda bb, i: (bb * nt + i, 0)),
            pl.BlockSpec((CONV_HALO, 2 * C_WIDTH),
                         lambda bb, i: (jnp.maximum((bb * nt + i) * hb - 1, 0), 0)),
            pl.BlockSpec((CONV_HALO, 2 * C_WIDTH),
                         lambda bb, i: (jnp.minimum((bb * nt + i + 1) * hb, last), 0)),
            pl.BlockSpec((32, C_WIDTH), lambda bb, i: (0, 0)),
            row, row, row,
        ],
        out_specs=pl.BlockSpec((CONV_T, C_WIDTH), lambda bb, i: (bb * nt + i, 0)),
        out_shape=jax.ShapeDtypeStruct((T, C_WIDTH), BF16),
        scratch_shapes=[pltpu.VMEM((CONV_T + 2 * CONV_HALO, C_WIDTH), F32)],
        compiler_params=_cparams(("parallel", "arbitrary")),
        name="conv_branch",
    )(zc, zc, zc, w, b, lg, lb)


def _vt_prep_kernel(v_ref, vt_ref):
    v = v_ref[...].astype(F32)
    for h in range(N_HEADS_D):
        vt_ref[h] = v[:, h * D_V_DIFF:(h + 1) * D_V_DIFF].T.astype(BF16)


def _vt_prep(zq):
    T = zq.shape[0]
    w = N_HEADS_D * D_V_DIFF
    return pl.pallas_call(
        _vt_prep_kernel,
        grid=(T // TK,),
        in_specs=[pl.BlockSpec((TK, w), lambda i: (i, 2))],
        out_specs=pl.BlockSpec((N_HEADS_D, None, D_V_DIFF, TK), lambda i: (0, i, 0, 0)),
        out_shape=jax.ShapeDtypeStruct((N_HEADS_D, T // TK, D_V_DIFF, TK), BF16),
        compiler_params=_cparams(("parallel",)),
        name="vt_prep",
    )(zq)


def _diff_attn_kernel(slope_ref, q_ref, k_ref, vt_ref, lq1_ref, lk1_ref, lq2_ref, lk2_ref, sg_ref,
                      o_ref, m_ref, l_ref, acc_ref, *, lam_init):
    h = pl.program_id(1)
    i = pl.program_id(2)
    tq = q_ref.shape[0]
    nkv = vt_ref.shape[0]
    slope = slope_ref[h] * LOG2E

    q = q_ref[...]
    lane = lax.broadcasted_iota(jnp.int32, q.shape, 1)
    zero = jnp.zeros_like(q)
    qs = jnp.concatenate([jnp.where(lane < HEAD_DIM, q, zero),
                          jnp.where(lane >= HEAD_DIM, q, zero)], axis=0)
    rel = (lax.broadcasted_iota(jnp.int32, (TK, tq), 1)
           - lax.broadcasted_iota(jnp.int32, (TK, tq), 0)).astype(F32)
    _softmax_init(m_ref, l_ref, acc_ref)

    def body(j, carry):
        kt = k_ref[pl.ds(pl.multiple_of(j * TK, TK), TK), :]
        s = _dot_nt(kt, qs)
        base = (i * tq - j * TK).astype(F32)
        bias = slope * jnp.abs(rel + base)
        s = s - jnp.concatenate([bias, bias], axis=1)
        _softmax_step(s, lambda p: _dot(vt_ref[j], p), m_ref, l_ref, acc_ref)
        return carry

    lax.fori_loop(0, nkv, body, 0)

    lam = (jnp.exp(jnp.sum(lq1_ref[...] * lk1_ref[...], axis=-1, keepdims=True))
           - jnp.exp(jnp.sum(lq2_ref[...] * lk2_ref[...], axis=-1, keepdims=True)) + lam_init)
    o = acc_ref[...] / l_ref[...]
    od = (o[:, :tq] - lam * o[:, tq:]).T
    ms = jnp.mean(od * od, axis=-1, keepdims=True)
    od = od * lax.rsqrt(ms + EPS) * sg_ref[...] * (1.0 - lam_init)
    o_ref[...] = od.astype(o_ref.dtype)


def _diff_attn(zq, vt, lam_params, subln_g, B, L, lam_init):
    T = B * L
    tq = TQ_DIFF
    nq = L // tq
    nkv = L // TK
    nqs = 2 * tq
    slopes = jnp.asarray(ALIBI_SLOPES, F32)
    vec = pl.BlockSpec((1, HEAD_DIM), lambda b, h, i: (0, 0))
    return pl.pallas_call(
        functools.partial(_diff_attn_kernel, lam_init=lam_init),
        grid=(B, N_HEADS_D, nq),
        in_specs=[
            pl.BlockSpec(memory_space=pltpu.SMEM),
            pl.BlockSpec((tq, LANES), lambda b, h, i: (b * nq + i, h)),
            pl.BlockSpec((L, LANES), lambda b, h, i: (b, N_HEADS_D + h)),
            pl.BlockSpec((None, nkv, D_V_DIFF, TK), lambda b, h, i: (h, b, 0, 0)),
            vec, vec, vec, vec,
            pl.BlockSpec((1, D_V_DIFF), lambda b, h, i: (0, 0)),
        ],
        out_specs=pl.BlockSpec((tq, LANES), lambda b, h, i: (b * nq + i, h)),
        out_shape=jax.ShapeDtypeStruct((T, N_HEADS_D * D_V_DIFF), BF16),
        scratch_shapes=[pltpu.VMEM((1, nqs), F32), pltpu.VMEM((1, nqs), F32),
                        pltpu.VMEM((D_V_DIFF, nqs), F32)],
        compiler_params=_cparams(("parallel", "parallel", "arbitrary")),
        name="diff_attn",
    )(slopes, zq, zq, vt, *lam_params, subln_g)


def _rope_tables(L):
    t = np.arange(L)
    row = (t // GRID_W).astype(np.float64)
    col = (t % GRID_W).astype(np.float64)
    half = HEAD_DIM // 2
    inv = ROPE_THETA ** (-np.arange(0, half, 2, dtype=np.float64) / half)
    ang = np.concatenate([row[:, None] * inv, col[:, None] * inv], axis=-1)
    cos = np.cos(ang).astype(np.float32)
    sin = np.sin(ang).astype(np.float32)
    cos_h = np.concatenate([cos, cos], axis=-1)
    sin_h = np.concatenate([-sin, sin], axis=-1)
    return (jnp.asarray(np.tile(cos_h, (1, LANES // HEAD_DIM))),
            jnp.asarray(np.tile(sin_h, (1, LANES // HEAD_DIM))))


_DEINT = np.concatenate([np.arange(0, HEAD_DIM, 2), np.arange(1, HEAD_DIM, 2)])


def _prep_weights(p):
    c = SCALE * LOG2E
    w = {}
    wa = N_HEADS_A * HEAD_DIM
    wq = N_HEADS_B * HEAD_DIM
    wk = N_KV_B * HEAD_DIM
    w_ab = p['w_in_ab']
    qcols = (np.arange(N_HEADS_B)[:, None] * HEAD_DIM + _DEINT[None, :]).reshape(-1)
    kcols = (np.arange(N_KV_B)[:, None] * HEAD_DIM + _DEINT[None, :]).reshape(-1)
    w['w_in_ab'] = jnp.concatenate([
        w_ab[:, :, :wa] * c,
        w_ab[:, :, wa:3 * wa],
        w_ab[:, :, 3 * wa:3 * wa + wq][:, :, qcols],
        w_ab[:, :, 3 * wa + wq:3 * wa + wq + wk][:, :, kcols],
        w_ab[:, :, 3 * wa + wq + wk:],
    ], axis=-1).astype(BF16)
    w['gq'] = jnp.tile(p['qnorm_b'][:, _DEINT] * c, (1, N_HEADS_B))[:, None, :]
    w['gk'] = jnp.tile(p['knorm_b'][:, _DEINT], (1, N_KV_B))[:, None, :]
    w_cd = p['w_in_cd']
    wd = N_HEADS_D * 2 * HEAD_DIM
    w['w_in_cd'] = jnp.concatenate([
        w_cd[:, :, :2 * C_WIDTH],
        w_cd[:, :, 2 * C_WIDTH:2 * C_WIDTH + wd] * c,
        w_cd[:, :, 2 * C_WIDTH + wd:],
    ], axis=-1).astype(BF16)
    for name in ('w_out_ab', 'w_out_cd', 'w1', 'w3', 'w2'):
        w[name] = p[name].astype(BF16)
    seg = np.kron(np.eye(N_HEADS_B, dtype=np.float32), np.ones((HEAD_DIM, HEAD_DIM), np.float32))
    w['seg'] = jnp.asarray(seg, BF16)
    w['conv_w'] = jnp.pad(p['conv_w_c'], ((0, 0), (0, 32 - CONV_K), (0, 0)))
    return w


def _trunk(x3, mod, p, w, rope):
    B, L, _ = x3.shape
    x = x3.reshape(B * L, D_MODEL)
    wa = N_HEADS_A * HEAD_DIM
    for li in range(DEPTH):
        sh1, sc1, g1, sh2, sc2, g2 = [mod[li, :, k] for k in range(6)]
        j = li // 2
        if li % 2 == 0:
            za, zb = _inproj(x, p['norm_mix_g'][li], sc1, sh1, w['w_in_ab'][j], L,
                             (3 * wa, (N_HEADS_B + 2 * N_KV_B) * HEAD_DIM), (BF16, F32))
            oa = _na_attn(za, _na_bias_table(p['rpb_a'][j]), B, L)
            q, k, vt = _gqa_prep(zb, rope[0], rope[1], w['gq'][j], w['gk'][j], w['seg'], L)
            ob = _gqa_attn(q, k, vt, B, L)
            x = _outproj(x, oa, ob, g1, w['w_out_ab'][j], L)
        else:
            zc, zq = _inproj(x, p['norm_mix_g'][li], sc1, sh1, w['w_in_cd'][j], L,
                             (2 * C_WIDTH, 3 * N_HEADS_D * D_V_DIFF), (F32, BF16))
            u = _conv_branch(zc, w['conv_w'][j], p['conv_b_c'][j][None], p['conv_ln_g'][j][None],
                             p['conv_ln_b'][j][None], B, L)
            lam_init = 0.8 - 0.6 * math.exp(-0.3 * li)
            lam_params = [p[n][j][None] for n in ('lam_q1', 'lam_k1', 'lam_q2', 'lam_k2')]
            od = _diff_attn(zq, _vt_prep(zq), lam_params, p['subln_g'][j][None], B, L, lam_init)
            x = _outproj(x, u, od, g1, w['w_out_cd'][j], L)
        x = _ffn(x, p['norm_ffn_g'][li], sc2, sh2, g2, w['w1'][li], w['w3'][li], w['w2'][li],
                 p['final_g'], L, final=(li == DEPTH - 1))
    return x.reshape(B, L, D_MODEL)


def kernel(x_prompt, x_sample, c_prompt, c_sample, w_mod, b_mod, norm_mix_g, norm_ffn_g, w_in_ab, rpb_a,
           qnorm_b, knorm_b, w_out_ab, w_in_cd, conv_w_c, conv_b_c, conv_ln_g, conv_ln_b, lam_q1, lam_k1,
           lam_q2, lam_k2, subln_g, w_out_cd, w1, w3, w2, final_g):
    p = dict(norm_mix_g=norm_mix_g, norm_ffn_g=norm_ffn_g, w_in_ab=w_in_ab, rpb_a=rpb_a, qnorm_b=qnorm_b,
             knorm_b=knorm_b, w_out_ab=w_out_ab, w_in_cd=w_in_cd, conv_w_c=conv_w_c, conv_b_c=conv_b_c,
             conv_ln_g=conv_ln_g, conv_ln_b=conv_ln_b, lam_q1=lam_q1, lam_k1=lam_k1, lam_q2=lam_q2,
             lam_k2=lam_k2, subln_g=subln_g, w_out_cd=w_out_cd, w1=w1, w3=w3, w2=w2, final_g=final_g)
    w = _prep_weights(p)
    bp, bs = c_prompt.shape[0], c_sample.shape[0]
    rows = -(-(bp + bs) // 8) * 8
    c_all = jnp.concatenate([c_prompt, c_sample, jnp.zeros((rows - bp - bs, D_MODEL), F32)], axis=0)
    mod = _modulation(c_all, w_mod, b_mod).reshape(DEPTH, rows, 6, 1, D_MODEL)
    y_prompt = _trunk(x_prompt, mod[:, :bp], p, w, _rope_tables(x_prompt.shape[1]))
    y_sample = _trunk(x_sample, mod[:, bp:bp + bs], p, w, _rope_tables(x_sample.shape[1]))
    return (y_prompt, y_sample)
```

```python
import functools
import math

import numpy as np
import jax
import jax.numpy as jnp
from jax import lax
from jax.experimental import pallas as pl
from jax.experimental.pallas import tpu as pltpu

F32 = jnp.float32
BF16 = jnp.bfloat16

D_MODEL = 1024
DEPTH = 4
GRID_W = 64
HEAD_DIM = 64
N_HEADS_A = 8
NA_ROWS = 8
NA_COLS = 16
N_HEADS_B = 8
N_KV_B = 2
GROUP_B = N_HEADS_B // N_KV_B
ROPE_THETA = 10000.0
C_WIDTH = 512
CONV_K = 31
N_HEADS_D = 4
D_V_DIFF = 2 * HEAD_DIM
D_FF = 2816
EPS = 1e-6
LN_EPS = 1e-5
ALIBI_SLOPES = tuple(2.0 ** (-8.0 * (h + 1) / N_HEADS_D) for h in range(N_HEADS_D))
LOG2E = math.log2(math.e)
SCALE = HEAD_DIM ** -0.5
NEG_BIG = -1e30

LANES = 128
VMEM_LIMIT = 56 * 1024 * 1024

TM = 512
TM_FFN = 1024
TN_FFN = 1408
TK = 512
TQ_GQA = 256
VPAD = 16
TQ_DIFF = 256
NA_R = 8
CONV_T = 256
CONV_HALO = 16
CONV_CHUNK = 64


def _cparams(sem):
    return pltpu.CompilerParams(dimension_semantics=sem, vmem_limit_bytes=VMEM_LIMIT)


def _dot(a, b):
    return jnp.dot(a, b, preferred_element_type=F32)


def _dot_nt(a, b):
    return lax.dot_general(a, b, (((1,), (1,)), ((), ())), preferred_element_type=F32)


def _silu(x):
    return x * jax.nn.sigmoid(x)


def _rms_mod(x, g, sc, sh):
    ms = jnp.mean(x * x, axis=-1, keepdims=True)
    h = x * lax.rsqrt(ms + EPS) * g
    return h * (1.0 + sc) + sh


def _mod_kernel(c_ref, w_ref, b_ref, o_ref):
    cs = _silu(c_ref[...])
    o_ref[...] = _dot(cs.astype(BF16), w_ref[...].astype(BF16)) + b_ref[...]


def _modulation(c_all, w_mod, b_mod):
    R = c_all.shape[0]
    tn = 1536
    return pl.pallas_call(
        _mod_kernel,
        grid=(DEPTH, 6 * D_MODEL // tn),
        in_specs=[
            pl.BlockSpec((R, D_MODEL), lambda l, j: (0, 0)),
            pl.BlockSpec((None, D_MODEL, tn), lambda l, j: (l, 0, j)),
            pl.BlockSpec((None, 1, tn), lambda l, j: (l, 0, j)),
        ],
        out_specs=pl.BlockSpec((None, R, tn), lambda l, j: (l, 0, j)),
        out_shape=jax.ShapeDtypeStruct((DEPTH, R, 6 * D_MODEL), F32),
        compiler_params=_cparams(("arbitrary", "arbitrary")),
        name="modulation",
    )(c_all, w_mod, b_mod.reshape(DEPTH, 1, 6 * D_MODEL))


def _inproj_kernel(x_ref, g_ref, sc_ref, sh_ref, w_ref, *o_refs, splits):
    h = _rms_mod(x_ref[...], g_ref[...], sc_ref[...], sh_ref[...])
    z = _dot(h.astype(BF16), w_ref[...])
    off = 0
    for o_ref, n in zip(o_refs, splits):
        o_ref[...] = z[:, off:off + n].astype(o_ref.dtype)
        off += n


def _inproj(x, g, sc, sh, w, L, splits, dtypes):
    T = x.shape[0]
    N = w.shape[1]
    per_b = L // TM
    vec = pl.BlockSpec((None, 1, D_MODEL), lambda i: (i // per_b, 0, 0))
    return pl.pallas_call(
        functools.partial(_inproj_kernel, splits=splits),
        grid=(T // TM,),
        in_specs=[
            pl.BlockSpec((TM, D_MODEL), lambda i: (i, 0)),
            pl.BlockSpec((1, D_MODEL), lambda i: (0, 0)),
            vec, vec,
            pl.BlockSpec((D_MODEL, N), lambda i: (0, 0)),
        ],
        out_specs=[pl.BlockSpec((TM, n), lambda i: (i, 0)) for n in splits],
        out_shape=[jax.ShapeDtypeStruct((T, n), dt) for n, dt in zip(splits, dtypes)],
        compiler_params=_cparams(("parallel",)),
        name="inproj",
    )(x, g.reshape(1, D_MODEL), sc, sh, w)


def _outproj_kernel(x_ref, a_ref, b_ref, gate_ref, w_ref, o_ref):
    half = a_ref.shape[1]
    y = _dot(a_ref[...], w_ref[:half, :]) + _dot(b_ref[...], w_ref[half:, :])
    o_ref[...] = x_ref[...] + (1.0 + gate_ref[...]) * y


def _outproj(x, a, b, gate, w, L):
    T = x.shape[0]
    per_b = L // TM
    half = a.shape[1]
    return pl.pallas_call(
        _outproj_kernel,
        grid=(T // TM,),
        in_specs=[
            pl.BlockSpec((TM, D_MODEL), lambda i: (i, 0)),
            pl.BlockSpec((TM, half), lambda i: (i, 0)),
            pl.BlockSpec((TM, half), lambda i: (i, 0)),
            pl.BlockSpec((None, 1, D_MODEL), lambda i: (i // per_b, 0, 0)),
            pl.BlockSpec((2 * half, D_MODEL), lambda i: (0, 0)),
        ],
        out_specs=pl.BlockSpec((TM, D_MODEL), lambda i: (i, 0)),
        out_shape=jax.ShapeDtypeStruct((T, D_MODEL), F32),
        compiler_params=_cparams(("parallel",)),
        name="outproj",
    )(x, a, b, gate, w)


def _ffn_kernel(x_ref, g_ref, sc_ref, sh_ref, gate_ref, w1_ref, w3_ref, w2_ref, fg_ref, o_ref,
                h_scr, acc_scr, *, final):
    j = pl.program_id(1)

    @pl.when(j == 0)
    def _():
        h_scr[...] = _rms_mod(x_ref[...], g_ref[...], sc_ref[...], sh_ref[...]).astype(BF16)
        acc_scr[...] = jnp.zeros_like(acc_scr)

    h = h_scr[...]
    t = _silu(_dot(h, w1_ref[...])) * _dot(h, w3_ref[...])
    acc_scr[...] += _dot(t.astype(BF16), w2_ref[...])

    @pl.when(j == pl.num_programs(1) - 1)
    def _():
        y = x_ref[...] + (1.0 + gate_ref[...]) * acc_scr[...]
        if final:
            ms = jnp.mean(y * y, axis=-1, keepdims=True)
            y = y * lax.rsqrt(ms + EPS) * fg_ref[...]
        o_ref[...] = y


def _ffn(x, g, sc, sh, gate, w1, w3, w2, final_g, L, final):
    T = x.shape[0]
    tm = min(TM_FFN, L)
    per_b = L // tm
    vec = pl.BlockSpec((None, 1, D_MODEL), lambda i, j: (i // per_b, 0, 0))
    row = pl.BlockSpec((1, D_MODEL), lambda i, j: (0, 0))
    return pl.pallas_call(
        functools.partial(_ffn_kernel, final=final),
        grid=(T // tm, D_FF // TN_FFN),
        in_specs=[
            pl.BlockSpec((tm, D_MODEL), lambda i, j: (i, 0)),
            row, vec, vec, vec,
            pl.BlockSpec((D_MODEL, TN_FFN), lambda i, j: (0, j)),
            pl.BlockSpec((D_MODEL, TN_FFN), lambda i, j: (0, j)),
            pl.BlockSpec((TN_FFN, D_MODEL), lambda i, j: (j, 0)),
            row,
        ],
        out_specs=pl.BlockSpec((tm, D_MODEL), lambda i, j: (i, 0)),
        out_shape=jax.ShapeDtypeStruct((T, D_MODEL), F32),
        scratch_shapes=[pltpu.VMEM((tm, D_MODEL), BF16), pltpu.VMEM((tm, D_MODEL), F32)],
        compiler_params=_cparams(("parallel", "arbitrary")),
        name="ffn",
    )(x, g.reshape(1, D_MODEL), sc, sh, gate, w1, w3, w2, final_g.reshape(1, D_MODEL))


def _flash_loop(nkv, score_fn, vt_ref, s_ref, mc_ref, m_ref, acc_ref):
    assert nkv % 2 == 0

    def produce(j, slot):
        s = score_fn(j)
        s_ref[slot] = s
        mc_ref[slot] = jnp.max(s, axis=0, keepdims=True)

    def consume(j, slot):
        m_prev = m_ref[...]
        m_new = jnp.maximum(m_prev, mc_ref[slot])
        alpha = jnp.exp2(m_prev - m_new)
        p = jnp.exp2(s_ref[slot] - m_new).astype(BF16)
        acc_ref[...] = alpha * acc_ref[...] + _dot(vt_ref[j], p)
        m_ref[...] = m_new

    m_ref[...] = jnp.full_like(m_ref, NEG_BIG)
    acc_ref[...] = jnp.zeros_like(acc_ref)
    produce(0, 0)

    def body(jj, carry):
        j = 2 * jj
        produce(j + 1, 1)
        consume(j, 0)
        produce(j + 2, 0)
        consume(j + 1, 1)
        return carry

    lax.fori_loop(0, nkv // 2 - 1, body, 0)
    produce(nkv - 1, 1)
    consume(nkv - 2, 0)
    consume(nkv - 1, 1)


def _seg_mean_sq(x, seg_ref):
    sq = x * x
    hi = sq.astype(BF16)
    lo = (sq - hi.astype(F32)).astype(BF16)
    n = x.shape[1]
    seg = seg_ref[:n, :n]
    return (_dot(hi, seg) + _dot(lo, seg)) * (1.0 / HEAD_DIM)


def _rope(x, cos, sin_signed, first_half):
    n = x.shape[1]
    half = HEAD_DIM // 2
    rot = jnp.where(first_half, pltpu.roll(x, n - half, 1), pltpu.roll(x, half, 1))
    return x * cos + rot * sin_signed


def _gqa_prep_kernel(z_ref, cos_ref, sin_ref, gq_ref, gk_ref, seg_ref, q_ref, k_ref, vt_ref):
    wq = N_HEADS_B * HEAD_DIM
    wk = N_KV_B * HEAD_DIM
    z = z_ref[...]
    cos2 = cos_ref[...]
    sin2 = sin_ref[...]
    tm = z.shape[0]

    def first_half(width):
        lane = lax.broadcasted_iota(jnp.int32, (tm, width), 1)
        return (lane % HEAD_DIM) < (HEAD_DIM // 2)

    xq = z[:, :wq]
    xq = xq * lax.rsqrt(_seg_mean_sq(xq, seg_ref) + EPS) * gq_ref[...]
    cosq = jnp.concatenate([cos2] * (wq // LANES), axis=1)
    sinq = jnp.concatenate([sin2] * (wq // LANES), axis=1)
    q = _rope(xq, cosq, sinq, first_half(wq))
    for h in range(N_HEADS_B):
        q_ref[h] = q[:, h * HEAD_DIM:(h + 1) * HEAD_DIM].astype(BF16)

    xk = z[:, wq:wq + wk]
    xk = xk * lax.rsqrt(_seg_mean_sq(xk, seg_ref) + EPS) * gk_ref[...]
    k = _rope(xk, cos2, sin2, first_half(wk))
    for g in range(N_KV_B):
        k_ref[g] = k[:, g * HEAD_DIM:(g + 1) * HEAD_DIM].astype(BF16)

    vt = z[:, wq + wk:].T
    pad = _ones_row_pad(tm)
    for g in range(N_KV_B):
        vt_ref[g] = jnp.concatenate([vt[g * HEAD_DIM:(g + 1) * HEAD_DIM, :], pad], axis=0).astype(BF16)


def _gqa_prep(zb, cos2, sin2, gq, gk, seg, L):
    T = zb.shape[0]
    per_b = L // TK
    wq = N_HEADS_B * HEAD_DIM
    wk = N_KV_B * HEAD_DIM
    return pl.pallas_call(
        _gqa_prep_kernel,
        grid=(T // TK,),
        in_specs=[
            pl.BlockSpec((TK, wq + 2 * wk), lambda i: (i, 0)),
            pl.BlockSpec((TK, LANES), lambda i: (i % per_b, 0)),
            pl.BlockSpec((TK, LANES), lambda i: (i % per_b, 0)),
            pl.BlockSpec((1, wq), lambda i: (0, 0)),
            pl.BlockSpec((1, wk), lambda i: (0, 0)),
            pl.BlockSpec((wq, wq), lambda i: (0, 0)),
        ],
        out_specs=[
            pl.BlockSpec((N_HEADS_B, TK, HEAD_DIM), lambda i: (0, i, 0)),
            pl.BlockSpec((N_KV_B, TK, HEAD_DIM), lambda i: (0, i, 0)),
            pl.BlockSpec((N_KV_B, None, HEAD_DIM + VPAD, TK), lambda i: (0, i, 0, 0)),
        ],
        out_shape=[
            jax.ShapeDtypeStruct((N_HEADS_B, T, HEAD_DIM), BF16),
            jax.ShapeDtypeStruct((N_KV_B, T, HEAD_DIM), BF16),
            jax.ShapeDtypeStruct((N_KV_B, T // TK, HEAD_DIM + VPAD, TK), BF16),
        ],
        compiler_params=_cparams(("parallel",)),
        name="gqa_prep",
    )(zb, cos2, sin2, gq, gk, seg)


def _ones_row_pad(n):
    row = lax.broadcasted_iota(jnp.int32, (VPAD, n), 0)
    return jnp.where(row == 0, 1.0, 0.0).astype(F32)


def _gqa_attn_kernel(q_ref, k_ref, vt_ref, o_ref, s_ref, mc_ref, m_ref, acc_ref):
    tq = q_ref.shape[1]
    nkv = vt_ref.shape[0]

    def scores(j):
        kt = k_ref[pl.ds(pl.multiple_of(j * TK, TK), TK), :]
        return _dot_nt(kt, q_ref[...].reshape(GROUP_B * tq, HEAD_DIM))

    _flash_loop(nkv, scores, vt_ref, s_ref, mc_ref, m_ref, acc_ref)
    acc = acc_ref[...]
    o = acc[:HEAD_DIM] / acc[HEAD_DIM:HEAD_DIM + 1]
    for pair in range(GROUP_B // 2):
        two = jnp.concatenate([o[:, (2 * pair) * tq:(2 * pair + 1) * tq],
                               o[:, (2 * pair + 1) * tq:(2 * pair + 2) * tq]], axis=0)
        o_ref[:, pair * LANES:(pair + 1) * LANES] = two.T.astype(o_ref.dtype)


def _gqa_attn(q, k, vt, B, L):
    T = B * L
    tq = TQ_GQA
    nq = L // tq
    nkv = L // TK
    nqs = GROUP_B * tq
    dv = HEAD_DIM + VPAD
    return pl.pallas_call(
        _gqa_attn_kernel,
        grid=(B, N_KV_B, nq),
        in_specs=[
            pl.BlockSpec((GROUP_B, tq, HEAD_DIM), lambda b, g, i: (g, b * nq + i, 0)),
            pl.BlockSpec((None, L, HEAD_DIM), lambda b, g, i: (g, b, 0)),
            pl.BlockSpec((None, nkv, dv, TK), lambda b, g, i: (g, b, 0, 0)),
        ],
        out_specs=pl.BlockSpec((tq, GROUP_B * HEAD_DIM), lambda b, g, i: (b * nq + i, g)),
        out_shape=jax.ShapeDtypeStruct((T, N_HEADS_B * HEAD_DIM), BF16),
        scratch_shapes=[pltpu.VMEM((2, TK, nqs), F32), pltpu.VMEM((2, 1, nqs), F32),
                        pltpu.VMEM((1, nqs), F32), pltpu.VMEM((dv, nqs), F32)],
        compiler_params=_cparams(("parallel", "parallel", "arbitrary")),
        name="gqa_attn",
    )(q, k, vt)


def _na_kernel(q_ref, k_ref, v_ref, bias_ref, o_ref, *, rows):
    i = pl.program_id(2)
    lane = lax.broadcasted_iota(jnp.int32, (GRID_W, LANES), 1)
    nwin = NA_ROWS * GRID_W

    def row(rr, carry):
        r = i * NA_R + rr
        rs = jnp.clip(r - NA_ROWS // 2, 0, rows - NA_ROWS)
        delta = r - rs
        q = q_ref[pl.ds(pl.multiple_of(rr * GRID_W, GRID_W), GRID_W), :]
        kw = k_ref[pl.ds(pl.multiple_of(rs * GRID_W, GRID_W), nwin), :]
        vw = v_ref[pl.ds(pl.multiple_of(rs * GRID_W, GRID_W), nwin), :]
        outs = []
        for hh in range(2):
            in_head = (lane >= hh * HEAD_DIM) & (lane < (hh + 1) * HEAD_DIM)
            qm = jnp.where(in_head, q, jnp.zeros_like(q))
            s = _dot_nt(qm, kw) + bias_ref[hh, delta]
            m = jnp.max(s, axis=-1, keepdims=True)
            p = jnp.exp2(s - m)
            l = jnp.sum(p, axis=-1, keepdims=True)
            outs.append(_dot(p.astype(BF16), vw) / l)
        o = jnp.where(lane < HEAD_DIM, outs[0], outs[1])
        o_ref[pl.ds(pl.multiple_of(rr * GRID_W, GRID_W), GRID_W), :] = o.astype(o_ref.dtype)
        return carry

    lax.fori_loop(0, NA_R, row, 0)


def _na_attn(za, bias, B, L):
    T = B * L
    rows = L // GRID_W
    assert rows >= NA_ROWS and rows % NA_R == 0
    nr = rows // NA_R
    blk = NA_R * GRID_W
    npair = N_HEADS_A // 2
    return pl.pallas_call(
        functools.partial(_na_kernel, rows=rows),
        grid=(B, npair, nr),
        in_specs=[
            pl.BlockSpec((blk, LANES), lambda b, hp, i: (b * nr + i, hp)),
            pl.BlockSpec((L, LANES), lambda b, hp, i: (b, npair + hp)),
            pl.BlockSpec((L, LANES), lambda b, hp, i: (b, 2 * npair + hp)),
            pl.BlockSpec((2, NA_ROWS, GRID_W, NA_ROWS * GRID_W), lambda b, hp, i: (hp, 0, 0, 0)),
        ],
        out_specs=pl.BlockSpec((blk, LANES), lambda b, hp, i: (b * nr + i, hp)),
        out_shape=jax.ShapeDtypeStruct((T, N_HEADS_A * HEAD_DIM), BF16),
        compiler_params=_cparams(("parallel", "parallel", "arbitrary")),
        name="na_attn",
    )(za, za, za, bias)


def _na_bias_table(rpb):
    col = np.arange(GRID_W)
    cs = np.clip(col - NA_COLS // 2, 0, GRID_W - NA_COLS)
    kc = np.arange(GRID_W)
    valid = (kc[None, :] >= cs[:, None]) & (kc[None, :] < cs[:, None] + NA_COLS)
    dcol = kc[None, :] - col[:, None] + (NA_COLS - 1)
    sel = ((dcol[None] == np.arange(2 * NA_COLS - 1)[:, None, None]) & valid[None]).astype(np.float32)
    band = jnp.einsum('hrd,dck->hrck', rpb.astype(F32), jnp.asarray(sel),
                      precision=lax.Precision.HIGHEST) * LOG2E
    band = jnp.where(valid[None, None], band, NEG_BIG)
    t = jnp.stack([band[:, NA_ROWS - 1 - d:2 * NA_ROWS - 1 - d] for d in range(NA_ROWS)], axis=1)
    t = t.transpose(0, 1, 3, 2, 4)
    return t.reshape(N_HEADS_A, NA_ROWS, GRID_W, NA_ROWS * GRID_W).astype(F32)


def _conv_kernel(z_ref, zp_ref, zn_ref, w_ref, b_ref, lg_ref, lb_ref, o_ref, u_scr):
    i = pl.program_id(1)
    nt = pl.num_programs(1)
    T = z_ref.shape[0]

    def glu(z):
        return z[:, :C_WIDTH] * jax.nn.sigmoid(z[:, C_WIDTH:])

    u_scr[CONV_HALO:CONV_HALO + T, :] = glu(z_ref[...])
    u_scr[:CONV_HALO, :] = jnp.where(i == 0, 0.0, glu(zp_ref[...]))
    u_scr[CONV_HALO + T:, :] = jnp.where(i == nt - 1, 0.0, glu(zn_ref[...]))

    off = CONV_HALO - CONV_K // 2
    for c in range(T // CONV_CHUNK):
        acc = jnp.zeros((CONV_CHUNK, C_WIDTH), F32)
        for k in range(CONV_K):
            acc = acc + w_ref[k:k + 1, :] * u_scr[pl.ds(c * CONV_CHUNK + off + k, CONV_CHUNK), :]
        y = acc + b_ref[...]
        mu = jnp.mean(y, axis=-1, keepdims=True)
        yc = y - mu
        var = jnp.mean(yc * yc, axis=-1, keepdims=True)
        y = yc * lax.rsqrt(var + LN_EPS) * lg_ref[...] + lb_ref[...]
        o_ref[c * CONV_CHUNK:(c + 1) * CONV_CHUNK, :] = _silu(y).astype(o_ref.dtype)


def _conv_branch(zc, w, b, lg, lb, B, L):
    T = B * L
    nt = L // CONV_T
    hb = CONV_T // CONV_HALO
    last = T // CONV_HALO - 1
    row = pl.BlockSpec((1, C_WIDTH), lambda bb, i: (0, 0))
    return pl.pallas_call(
        _conv_kernel,
        grid=(B, nt),
        in_specs=[
            pl.BlockSpec((CONV_T, 2 * C_WIDTH), lambda bb, i: (bb * nt + i, 0)),
            pl.BlockSpec((CONV_HALO, 2 * C_WIDTH),
                         lambda bb, i: (jnp.maximum((bb * nt + i) * hb - 1, 0), 0)),
            pl.BlockSpec((CONV_HALO, 2 * C_WIDTH),
                         lambda bb, i: (jnp.minimum((bb * nt + i + 1) * hb, last), 0)),
            pl.BlockSpec((32, C_WIDTH), lambda bb, i: (0, 0)),
            row, row, row,
        ],
        out_specs=pl.BlockSpec((CONV_T, C_WIDTH), lambda bb, i: (bb * nt + i, 0)),
        out_shape=jax.ShapeDtypeStruct((T, C_WIDTH), BF16),
        scratch_shapes=[pltpu.VMEM((CONV_T + 2 * CONV_HALO, C_WIDTH), F32)],
        compiler_params=_cparams(("parallel", "arbitrary")),
        name="conv_branch",
    )(zc, zc, zc, w, b, lg, lb)


def _vt_prep_kernel(v_ref, vt_ref):
    v = v_ref[...].astype(F32)
    pad = _ones_row_pad(v.shape[0])
    for h in range(N_HEADS_D):
        vt = v[:, h * D_V_DIFF:(h + 1) * D_V_DIFF].T
        vt_ref[h] = jnp.concatenate([vt, pad], axis=0).astype(BF16)


def _vt_prep(zq):
    T = zq.shape[0]
    w = N_HEADS_D * D_V_DIFF
    dv = D_V_DIFF + VPAD
    return pl.pallas_call(
        _vt_prep_kernel,
        grid=(T // TK,),
        in_specs=[pl.BlockSpec((TK, w), lambda i: (i, 2))],
        out_specs=pl.BlockSpec((N_HEADS_D, None, dv, TK), lambda i: (0, i, 0, 0)),
        out_shape=jax.ShapeDtypeStruct((N_HEADS_D, T // TK, dv, TK), BF16),
        compiler_params=_cparams(("parallel",)),
        name="vt_prep",
    )(zq)


def _diff_attn_kernel(slope_ref, q_ref, k_ref, vt_ref, lq1_ref, lk1_ref, lq2_ref, lk2_ref, sg_ref,
                      o_ref, qs_ref, rel_ref, s_ref, mc_ref, m_ref, acc_ref, *, lam_init):
    h = pl.program_id(1)
    i = pl.program_id(2)
    tq = q_ref.shape[0]
    nkv = vt_ref.shape[0]
    slope = slope_ref[h] * LOG2E

    q = q_ref[...]
    lane = lax.broadcasted_iota(jnp.int32, q.shape, 1)
    zero = jnp.zeros_like(q)
    qs_ref[:tq, :] = jnp.where(lane < HEAD_DIM, q, zero)
    qs_ref[tq:, :] = jnp.where(lane >= HEAD_DIM, q, zero)
    rel_ref[...] = (lax.broadcasted_iota(jnp.int32, (TK, tq), 1)
                    - lax.broadcasted_iota(jnp.int32, (TK, tq), 0)).astype(F32)

    def scores(j):
        kt = k_ref[pl.ds(pl.multiple_of(j * TK, TK), TK), :]
        s = _dot_nt(kt, qs_ref[...])
        base = (i * tq - j * TK).astype(F32)
        bias = slope * jnp.abs(rel_ref[...] + base)
        return s - jnp.concatenate([bias, bias], axis=1)

    _flash_loop(nkv, scores, vt_ref, s_ref, mc_ref, m_ref, acc_ref)

    lam = (jnp.exp(jnp.sum(lq1_ref[...] * lk1_ref[...], axis=-1, keepdims=True))
           - jnp.exp(jnp.sum(lq2_ref[...] * lk2_ref[...], axis=-1, keepdims=True)) + lam_init)
    acc = acc_ref[...]
    o = acc[:D_V_DIFF] / acc[D_V_DIFF:D_V_DIFF + 1]
    od = (o[:, :tq] - lam * o[:, tq:]).T
    ms = jnp.mean(od * od, axis=-1, keepdims=True)
    od = od * lax.rsqrt(ms + EPS) * sg_ref[...] * (1.0 - lam_init)
    o_ref[...] = od.astype(o_ref.dtype)


def _diff_attn(zq, vt, lam_params, subln_g, B, L, lam_init):
    T = B * L
    tq = TQ_DIFF
    nq = L // tq
    nkv = L // TK
    nqs = 2 * tq
    dv = D_V_DIFF + VPAD
    slopes = jnp.asarray(ALIBI_SLOPES, F32)
    vec = pl.BlockSpec((1, HEAD_DIM), lambda b, h, i: (0, 0))
    return pl.pallas_call(
        functools.partial(_diff_attn_kernel, lam_init=lam_init),
        grid=(B, N_HEADS_D, nq),
        in_specs=[
            pl.BlockSpec(memory_space=pltpu.SMEM),
            pl.BlockSpec((tq, LANES), lambda b, h, i: (b * nq + i, h)),
            pl.BlockSpec((L, LANES), lambda b, h, i: (b, N_HEADS_D + h)),
            pl.BlockSpec((None, nkv, dv, TK), lambda b, h, i: (h, b, 0, 0)),
            vec, vec, vec, vec,
            pl.BlockSpec((1, D_V_DIFF), lambda b, h, i: (0, 0)),
        ],
        out_specs=pl.BlockSpec((tq, LANES), lambda b, h, i: (b * nq + i, h)),
        out_shape=jax.ShapeDtypeStruct((T, N_HEADS_D * D_V_DIFF), BF16),
        scratch_shapes=[pltpu.VMEM((nqs, LANES), BF16), pltpu.VMEM((TK, tq), F32),
                        pltpu.VMEM((2, TK, nqs), F32), pltpu.VMEM((2, 1, nqs), F32),
                        pltpu.VMEM((1, nqs), F32), pltpu.VMEM((dv, nqs), F32)],
        compiler_params=_cparams(("parallel", "parallel", "arbitrary")),
        name="diff_attn",
    )(slopes, zq, zq, vt, *lam_params, subln_g)


def _rope_tables(L):
    t = np.arange(L)
    row = (t // GRID_W).astype(np.float64)
    col = (t % GRID_W).astype(np.float64)
    half = HEAD_DIM // 2
    inv = ROPE_THETA ** (-np.arange(0, half, 2, dtype=np.float64) / half)
    ang = np.concatenate([row[:, None] * inv, col[:, None] * inv], axis=-1)
    cos = np.cos(ang).astype(np.float32)
    sin = np.sin(ang).astype(np.float32)
    cos_h = np.concatenate([cos, cos], axis=-1)
    sin_h = np.concatenate([-sin, sin], axis=-1)
    return (jnp.asarray(np.tile(cos_h, (1, LANES // HEAD_DIM))),
            jnp.asarray(np.tile(sin_h, (1, LANES // HEAD_DIM))))


_DEINT = np.concatenate([np.arange(0, HEAD_DIM, 2), np.arange(1, HEAD_DIM, 2)])


def _prep_weights(p):
    c = SCALE * LOG2E
    w = {}
    wa = N_HEADS_A * HEAD_DIM
    wq = N_HEADS_B * HEAD_DIM
    wk = N_KV_B * HEAD_DIM
    w_ab = p['w_in_ab']
    qcols = (np.arange(N_HEADS_B)[:, None] * HEAD_DIM + _DEINT[None, :]).reshape(-1)
    kcols = (np.arange(N_KV_B)[:, None] * HEAD_DIM + _DEINT[None, :]).reshape(-1)
    w['w_in_ab'] = jnp.concatenate([
        w_ab[:, :, :wa] * c,
        w_ab[:, :, wa:3 * wa],
        w_ab[:, :, 3 * wa:3 * wa + wq][:, :, qcols],
        w_ab[:, :, 3 * wa + wq:3 * wa + wq + wk][:, :, kcols],
        w_ab[:, :, 3 * wa + wq + wk:],
    ], axis=-1).astype(BF16)
    w['gq'] = jnp.tile(p['qnorm_b'][:, _DEINT] * c, (1, N_HEADS_B))[:, None, :]
    w['gk'] = jnp.tile(p['knorm_b'][:, _DEINT], (1, N_KV_B))[:, None, :]
    w_cd = p['w_in_cd']
    wd = N_HEADS_D * 2 * HEAD_DIM
    w['w_in_cd'] = jnp.concatenate([
        w_cd[:, :, :2 * C_WIDTH],
        w_cd[:, :, 2 * C_WIDTH:2 * C_WIDTH + wd] * c,
        w_cd[:, :, 2 * C_WIDTH + wd:],
    ], axis=-1).astype(BF16)
    for name in ('w_out_ab', 'w_out_cd', 'w1', 'w3', 'w2'):
        w[name] = p[name].astype(BF16)
    seg = np.kron(np.eye(N_HEADS_B, dtype=np.float32), np.ones((HEAD_DIM, HEAD_DIM), np.float32))
    w['seg'] = jnp.asarray(seg, BF16)
    w['na_bias'] = [_na_bias_table(p['rpb_a'][j]) for j in range(p['rpb_a'].shape[0])]
    w['conv_w'] = jnp.pad(p['conv_w_c'], ((0, 0), (0, 32 - CONV_K), (0, 0)))
    return w


def _trunk(x3, mod, p, w, rope):
    B, L, _ = x3.shape
    x = x3.reshape(B * L, D_MODEL)
    wa = N_HEADS_A * HEAD_DIM
    for li in range(DEPTH):
        sh1, sc1, g1, sh2, sc2, g2 = [mod[li, :, k] for k in range(6)]
        j = li // 2
        if li % 2 == 0:
            za, zb = _inproj(x, p['norm_mix_g'][li], sc1, sh1, w['w_in_ab'][j], L,
                             (3 * wa, (N_HEADS_B + 2 * N_KV_B) * HEAD_DIM), (BF16, F32))
            oa = _na_attn(za, w['na_bias'][j], B, L)
            q, k, vt = _gqa_prep(zb, rope[0], rope[1], w['gq'][j], w['gk'][j], w['seg'], L)
            ob = _gqa_attn(q, k, vt, B, L)
            x = _outproj(x, oa, ob, g1, w['w_out_ab'][j], L)
        else:
            zc, zq = _inproj(x, p['norm_mix_g'][li], sc1, sh1, w['w_in_cd'][j], L,
                             (2 * C_WIDTH, 3 * N_HEADS_D * D_V_DIFF), (F32, BF16))
            u = _conv_branch(zc, w['conv_w'][j], p['conv_b_c'][j][None], p['conv_ln_g'][j][None],
                             p['conv_ln_b'][j][None], B, L)
            lam_init = 0.8 - 0.6 * math.exp(-0.3 * li)
            lam_params = [p[n][j][None] for n in ('lam_q1', 'lam_k1', 'lam_q2', 'lam_k2')]
            od = _diff_attn(zq, _vt_prep(zq), lam_params, p['subln_g'][j][None], B, L, lam_init)
            x = _outproj(x, u, od, g1, w['w_out_cd'][j], L)
        x = _ffn(x, p['norm_ffn_g'][li], sc2, sh2, g2, w['w1'][li], w['w3'][li], w['w2'][li],
                 p['final_g'], L, final=(li == DEPTH - 1))
    return x.reshape(B, L, D_MODEL)


def kernel(x_prompt, x_sample, c_prompt, c_sample, w_mod, b_mod, norm_mix_g, norm_ffn_g, w_in_ab, rpb_a,
           qnorm_b, knorm_b, w_out_ab, w_in_cd, conv_w_c, conv_b_c, conv_ln_g, conv_ln_b, lam_q1, lam_k1,
           lam_q2, lam_k2, subln_g, w_out_cd, w1, w3, w2, final_g):
    p = dict(norm_mix_g=norm_mix_g, norm_ffn_g=norm_ffn_g, w_in_ab=w_in_ab, rpb_a=rpb_a, qnorm_b=qnorm_b,
             knorm_b=knorm_b, w_out_ab=w_out_ab, w_in_cd=w_in_cd, conv_w_c=conv_w_c, conv_b_c=conv_b_c,
             conv_ln_g=conv_ln_g, conv_ln_b=conv_ln_b, lam_q1=lam_q1, lam_k1=lam_k1, lam_q2=lam_q2,
             lam_k2=lam_k2, subln_g=subln_g, w_out_cd=w_out_cd, w1=w1, w3=w3, w2=w2, final_g=final_g)
    w = _prep_weights(p)
    bp, bs = c_prompt.shape[0], c_sample.shape[0]
    rows = -(-(bp + bs) // 8) * 8
    c_all = jnp.concatenate([c_prompt, c_sample, jnp.zeros((rows - bp - bs, D_MODEL), F32)], axis=0)
    mod = _modulation(c_all, w_mod, b_mod).reshape(DEPTH, rows, 6, 1, D_MODEL)
    y_prompt = _trunk(x_prompt, mod[:, :bp], p, w, _rope_tables(x_prompt.shape[1]))
    y_sample = _trunk(x_sample, mod[:, bp:bp + bs], p, w, _rope_tables(x_sample.shape[1]))
    return (y_prompt, y_sample)
```

```python
import functools
import math

import numpy as np
import jax
import jax.numpy as jnp
from jax import lax
from jax.experimental import pallas as pl
from jax.experimental.pallas import tpu as pltpu

F32 = jnp.float32
BF16 = jnp.bfloat16

D_MODEL = 1024
DEPTH = 4
GRID_W = 64
HEAD_DIM = 64
N_HEADS_A = 8
NA_ROWS = 8
NA_COLS = 16
N_HEADS_B = 8
N_KV_B = 2
GROUP_B = N_HEADS_B // N_KV_B
ROPE_THETA = 10000.0
C_WIDTH = 512
CONV_K = 31
N_HEADS_D = 4
D_V_DIFF = 2 * HEAD_DIM
D_FF = 2816
EPS = 1e-6
LN_EPS = 1e-5
ALIBI_SLOPES = tuple(2.0 ** (-8.0 * (h + 1) / N_HEADS_D) for h in range(N_HEADS_D))
LOG2E = math.log2(math.e)
SCALE = HEAD_DIM ** -0.5
NEG_BIG = -1e30

LANES = 128
VMEM_LIMIT = 56 * 1024 * 1024

TM = 512
TM_FFN = 1024
TN_FFN = 1408
TK = 512
TQ_GQA = 256
VPAD = 16
TQ_DIFF = TK
NA_R = 8
NA_WIN = 2 * NA_R
NA_BIG = 2.0 ** 100
CONV_T = 256
CONV_HALO = 16
CONV_CHUNK = 64


def _cparams(sem):
    return pltpu.CompilerParams(dimension_semantics=sem, vmem_limit_bytes=VMEM_LIMIT)


def _dot(a, b):
    return jnp.dot(a, b, preferred_element_type=F32)


def _dot_nt(a, b):
    return lax.dot_general(a, b, (((1,), (1,)), ((), ())), preferred_element_type=F32)


def _silu(x):
    return x * jax.nn.sigmoid(x)


def _rms_mod(x, g, sc, sh):
    ms = jnp.mean(x * x, axis=-1, keepdims=True)
    h = x * lax.rsqrt(ms + EPS) * g
    return h * (1.0 + sc) + sh


def _mod_kernel(c_ref, w_ref, b_ref, o_ref):
    cs = _silu(c_ref[...])
    o_ref[...] = _dot(cs.astype(BF16), w_ref[...].astype(BF16)) + b_ref[...]


def _modulation(c_all, w_mod, b_mod):
    R = c_all.shape[0]
    tn = 1536
    return pl.pallas_call(
        _mod_kernel,
        grid=(DEPTH, 6 * D_MODEL // tn),
        in_specs=[
            pl.BlockSpec((R, D_MODEL), lambda l, j: (0, 0)),
            pl.BlockSpec((None, D_MODEL, tn), lambda l, j: (l, 0, j)),
            pl.BlockSpec((None, 1, tn), lambda l, j: (l, 0, j)),
        ],
        out_specs=pl.BlockSpec((None, R, tn), lambda l, j: (l, 0, j)),
        out_shape=jax.ShapeDtypeStruct((DEPTH, R, 6 * D_MODEL), F32),
        compiler_params=_cparams(("arbitrary", "arbitrary")),
        name="modulation",
    )(c_all, w_mod, b_mod.reshape(DEPTH, 1, 6 * D_MODEL))


def _inproj_kernel(x_ref, g_ref, sc_ref, sh_ref, w_ref, *o_refs, splits):
    h = _rms_mod(x_ref[...], g_ref[...], sc_ref[...], sh_ref[...])
    z = _dot(h.astype(BF16), w_ref[...])
    off = 0
    for o_ref, n in zip(o_refs, splits):
        o_ref[...] = z[:, off:off + n].astype(o_ref.dtype)
        off += n


def _inproj(x, g, sc, sh, w, L, splits, dtypes):
    T = x.shape[0]
    N = w.shape[1]
    per_b = L // TM
    vec = pl.BlockSpec((None, 1, D_MODEL), lambda i: (i // per_b, 0, 0))
    return pl.pallas_call(
        functools.partial(_inproj_kernel, splits=splits),
        grid=(T // TM,),
        in_specs=[
            pl.BlockSpec((TM, D_MODEL), lambda i: (i, 0)),
            pl.BlockSpec((1, D_MODEL), lambda i: (0, 0)),
            vec, vec,
            pl.BlockSpec((D_MODEL, N), lambda i: (0, 0)),
        ],
        out_specs=[pl.BlockSpec((TM, n), lambda i: (i, 0)) for n in splits],
        out_shape=[jax.ShapeDtypeStruct((T, n), dt) for n, dt in zip(splits, dtypes)],
        compiler_params=_cparams(("parallel",)),
        name="inproj",
    )(x, g.reshape(1, D_MODEL), sc, sh, w)


def _outproj_kernel(x_ref, a_ref, b_ref, gate_ref, w_ref, o_ref):
    half = a_ref.shape[1]
    y = _dot(a_ref[...], w_ref[:half, :]) + _dot(b_ref[...], w_ref[half:, :])
    o_ref[...] = x_ref[...] + (1.0 + gate_ref[...]) * y


def _outproj(x, a, b, gate, w, L):
    T = x.shape[0]
    per_b = L // TM
    half = a.shape[1]
    return pl.pallas_call(
        _outproj_kernel,
        grid=(T // TM,),
        in_specs=[
            pl.BlockSpec((TM, D_MODEL), lambda i: (i, 0)),
            pl.BlockSpec((TM, half), lambda i: (i, 0)),
            pl.BlockSpec((TM, half), lambda i: (i, 0)),
            pl.BlockSpec((None, 1, D_MODEL), lambda i: (i // per_b, 0, 0)),
            pl.BlockSpec((2 * half, D_MODEL), lambda i: (0, 0)),
        ],
        out_specs=pl.BlockSpec((TM, D_MODEL), lambda i: (i, 0)),
        out_shape=jax.ShapeDtypeStruct((T, D_MODEL), F32),
        compiler_params=_cparams(("parallel",)),
        name="outproj",
    )(x, a, b, gate, w)


def _ffn_kernel(x_ref, g_ref, sc_ref, sh_ref, gate_ref, w1_ref, w3_ref, w2_ref, fg_ref, o_ref,
                h_scr, acc_scr, *, final):
    j = pl.program_id(1)

    @pl.when(j == 0)
    def _():
        h_scr[...] = _rms_mod(x_ref[...], g_ref[...], sc_ref[...], sh_ref[...]).astype(BF16)
        acc_scr[...] = jnp.zeros_like(acc_scr)

    h = h_scr[...]
    t = _silu(_dot(h, w1_ref[...])) * _dot(h, w3_ref[...])
    acc_scr[...] += _dot(t.astype(BF16), w2_ref[...])

    @pl.when(j == pl.num_programs(1) - 1)
    def _():
        y = x_ref[...] + (1.0 + gate_ref[...]) * acc_scr[...]
        if final:
            ms = jnp.mean(y * y, axis=-1, keepdims=True)
            y = y * lax.rsqrt(ms + EPS) * fg_ref[...]
        o_ref[...] = y


def _ffn(x, g, sc, sh, gate, w1, w3, w2, final_g, L, final):
    T = x.shape[0]
    tm = min(TM_FFN, L)
    per_b = L // tm
    vec = pl.BlockSpec((None, 1, D_MODEL), lambda i, j: (i // per_b, 0, 0))
    row = pl.BlockSpec((1, D_MODEL), lambda i, j: (0, 0))
    return pl.pallas_call(
        functools.partial(_ffn_kernel, final=final),
        grid=(T // tm, D_FF // TN_FFN),
        in_specs=[
            pl.BlockSpec((tm, D_MODEL), lambda i, j: (i, 0)),
            row, vec, vec, vec,
            pl.BlockSpec((D_MODEL, TN_FFN), lambda i, j: (0, j)),
            pl.BlockSpec((D_MODEL, TN_FFN), lambda i, j: (0, j)),
            pl.BlockSpec((TN_FFN, D_MODEL), lambda i, j: (j, 0)),
            row,
        ],
        out_specs=pl.BlockSpec((tm, D_MODEL), lambda i, j: (i, 0)),
        out_shape=jax.ShapeDtypeStruct((T, D_MODEL), F32),
        scratch_shapes=[pltpu.VMEM((tm, D_MODEL), BF16), pltpu.VMEM((tm, D_MODEL), F32)],
        compiler_params=_cparams(("parallel", "arbitrary")),
        name="ffn",
    )(x, g.reshape(1, D_MODEL), sc, sh, gate, w1, w3, w2, final_g.reshape(1, D_MODEL))


def _flash_loop(nkv, score_fn, tile_fn, offset_fn, vt_ref, s_ref, mc_ref, m_ref, acc_ref):
    assert nkv % 2 == 0

    def produce(t, slot, first=False):
        s = score_fn(t, first)
        s_ref[slot] = s
        mc_ref[slot] = jnp.max(s, axis=0, keepdims=True) + offset_fn(t)

    def consume(t, slot):
        m_prev = m_ref[...]
        m_new = jnp.maximum(m_prev, mc_ref[slot])
        alpha = jnp.exp2(m_prev - m_new)
        p = jnp.exp2(s_ref[slot] - (m_new - offset_fn(t))).astype(BF16)
        acc_ref[...] = alpha * acc_ref[...] + _dot(vt_ref[tile_fn(t)], p)
        m_ref[...] = m_new

    m_ref[...] = jnp.full_like(m_ref, NEG_BIG)
    acc_ref[...] = jnp.zeros_like(acc_ref)
    produce(0, 0, first=True)

    def body(tt, carry):
        t = 2 * tt
        produce(t + 1, 1)
        consume(t, 0)
        produce(t + 2, 0)
        consume(t + 1, 1)
        return carry

    lax.fori_loop(0, nkv // 2 - 1, body, 0)
    produce(nkv - 1, 1)
    consume(nkv - 2, 0)
    consume(nkv - 1, 1)


def _seg_mean_sq(x, seg_ref):
    sq = x * x
    hi = sq.astype(BF16)
    lo = (sq - hi.astype(F32)).astype(BF16)
    n = x.shape[1]
    seg = seg_ref[:n, :n]
    return (_dot(hi, seg) + _dot(lo, seg)) * (1.0 / HEAD_DIM)


def _rope(x, cos, sin_signed, first_half):
    n = x.shape[1]
    half = HEAD_DIM // 2
    rot = jnp.where(first_half, pltpu.roll(x, n - half, 1), pltpu.roll(x, half, 1))
    return x * cos + rot * sin_signed


def _gqa_prep_kernel(z_ref, cos_ref, sin_ref, gq_ref, gk_ref, seg_ref, q_ref, k_ref, vt_ref):
    wq = N_HEADS_B * HEAD_DIM
    wk = N_KV_B * HEAD_DIM
    z = z_ref[...]
    cos2 = cos_ref[...]
    sin2 = sin_ref[...]
    tm = z.shape[0]

    def first_half(width):
        lane = lax.broadcasted_iota(jnp.int32, (tm, width), 1)
        return (lane % HEAD_DIM) < (HEAD_DIM // 2)

    xq = z[:, :wq]
    xq = xq * lax.rsqrt(_seg_mean_sq(xq, seg_ref) + EPS) * gq_ref[...]
    cosq = jnp.concatenate([cos2] * (wq // LANES), axis=1)
    sinq = jnp.concatenate([sin2] * (wq // LANES), axis=1)
    q = _rope(xq, cosq, sinq, first_half(wq))
    for h in range(N_HEADS_B):
        q_ref[h] = q[:, h * HEAD_DIM:(h + 1) * HEAD_DIM].astype(BF16)

    xk = z[:, wq:wq + wk]
    xk = xk * lax.rsqrt(_seg_mean_sq(xk, seg_ref) + EPS) * gk_ref[...]
    k = _rope(xk, cos2, sin2, first_half(wk))
    for g in range(N_KV_B):
        k_ref[g] = k[:, g * HEAD_DIM:(g + 1) * HEAD_DIM].astype(BF16)

    vt = z[:, wq + wk:].T
    pad = _ones_row_pad(tm)
    for g in range(N_KV_B):
        vt_ref[g] = jnp.concatenate([vt[g * HEAD_DIM:(g + 1) * HEAD_DIM, :], pad], axis=0).astype(BF16)


def _gqa_prep(zb, cos2, sin2, gq, gk, seg, L):
    T = zb.shape[0]
    per_b = L // TK
    wq = N_HEADS_B * HEAD_DIM
    wk = N_KV_B * HEAD_DIM
    return pl.pallas_call(
        _gqa_prep_kernel,
        grid=(T // TK,),
        in_specs=[
            pl.BlockSpec((TK, wq + 2 * wk), lambda i: (i, 0)),
            pl.BlockSpec((TK, LANES), lambda i: (i % per_b, 0)),
            pl.BlockSpec((TK, LANES), lambda i: (i % per_b, 0)),
            pl.BlockSpec((1, wq), lambda i: (0, 0)),
            pl.BlockSpec((1, wk), lambda i: (0, 0)),
            pl.BlockSpec((wq, wq), lambda i: (0, 0)),
        ],
        out_specs=[
            pl.BlockSpec((N_HEADS_B, TK, HEAD_DIM), lambda i: (0, i, 0)),
            pl.BlockSpec((N_KV_B, TK, HEAD_DIM), lambda i: (0, i, 0)),
            pl.BlockSpec((N_KV_B, None, HEAD_DIM + VPAD, TK), lambda i: (0, i, 0, 0)),
        ],
        out_shape=[
            jax.ShapeDtypeStruct((N_HEADS_B, T, HEAD_DIM), BF16),
            jax.ShapeDtypeStruct((N_KV_B, T, HEAD_DIM), BF16),
            jax.ShapeDtypeStruct((N_KV_B, T // TK, HEAD_DIM + VPAD, TK), BF16),
        ],
        compiler_params=_cparams(("parallel",)),
        name="gqa_prep",
    )(zb, cos2, sin2, gq, gk, seg)


def _ones_row_pad(n):
    row = lax.broadcasted_iota(jnp.int32, (VPAD, n), 0)
    return jnp.where(row == 0, 1.0, 0.0).astype(F32)


def _gqa_attn_kernel(q_ref, k_ref, vt_ref, o_ref, s_ref, mc_ref, m_ref, acc_ref):
    tq = q_ref.shape[1]
    nkv = vt_ref.shape[0]

    def scores(t, first):
        kt = k_ref[pl.ds(pl.multiple_of(t * TK, TK), TK), :]
        return _dot_nt(kt, q_ref[...].reshape(GROUP_B * tq, HEAD_DIM))

    _flash_loop(nkv, scores, lambda t: t, lambda t: 0.0, vt_ref, s_ref, mc_ref, m_ref, acc_ref)
    acc = acc_ref[...]
    o = acc[:HEAD_DIM] / acc[HEAD_DIM:HEAD_DIM + 1]
    for pair in range(GROUP_B // 2):
        two = jnp.concatenate([o[:, (2 * pair) * tq:(2 * pair + 1) * tq],
                               o[:, (2 * pair + 1) * tq:(2 * pair + 2) * tq]], axis=0)
        o_ref[:, pair * LANES:(pair + 1) * LANES] = two.T.astype(o_ref.dtype)


def _gqa_attn(q, k, vt, B, L):
    T = B * L
    tq = TQ_GQA
    nq = L // tq
    nkv = L // TK
    nqs = GROUP_B * tq
    dv = HEAD_DIM + VPAD
    return pl.pallas_call(
        _gqa_attn_kernel,
        grid=(B, N_KV_B, nq),
        in_specs=[
            pl.BlockSpec((GROUP_B, tq, HEAD_DIM), lambda b, g, i: (g, b * nq + i, 0)),
            pl.BlockSpec((None, L, HEAD_DIM), lambda b, g, i: (g, b, 0)),
            pl.BlockSpec((None, nkv, dv, TK), lambda b, g, i: (g, b, 0, 0)),
        ],
        out_specs=pl.BlockSpec((tq, GROUP_B * HEAD_DIM), lambda b, g, i: (b * nq + i, g)),
        out_shape=jax.ShapeDtypeStruct((T, N_HEADS_B * HEAD_DIM), BF16),
        scratch_shapes=[pltpu.VMEM((2, TK, nqs), F32), pltpu.VMEM((2, 1, nqs), F32),
                        pltpu.VMEM((1, nqs), F32), pltpu.VMEM((dv, nqs), F32)],
        compiler_params=_cparams(("parallel", "parallel", "arbitrary")),
        name="gqa_attn",
    )(q, k, vt)


def _na_kernel(q_ref, k_ref, v_ref, g_ref, kx_ref, qx_ref, o_ref, *, rows):
    i = pl.program_id(2)
    nk = NA_WIN * GRID_W
    r0 = i * NA_R
    ws = jnp.clip(r0 - NA_ROWS // 2, 0, rows - NA_WIN)
    koff = pl.multiple_of(ws * GRID_W, GRID_W)
    goff = pl.multiple_of((NA_R - (r0 - ws)) * GRID_W, GRID_W)
    kaug = jnp.concatenate([k_ref[pl.ds(koff, nk), :], kx_ref[...]], axis=1)
    vt = jnp.concatenate([v_ref[pl.ds(koff, nk), :].astype(F32).T, _ones_row_pad(nk)], axis=0).astype(BF16)
    q = q_ref[...]
    lane = lax.broadcasted_iota(jnp.int32, q.shape, 1)
    outs = []
    for hh in range(2):
        in_head = (lane >= hh * HEAD_DIM) & (lane < (hh + 1) * HEAD_DIM)
        qaug = jnp.concatenate([jnp.where(in_head, q, jnp.zeros_like(q)), qx_ref[...]], axis=1)
        s = _dot_nt(kaug, qaug) + g_ref[hh, pl.ds(goff, nk), :]
        p = jnp.exp2(s - jnp.max(s, axis=0, keepdims=True)).astype(BF16)
        acc = _dot(vt, p)
        outs.append(acc[hh * HEAD_DIM:(hh + 1) * HEAD_DIM] / acc[2 * HEAD_DIM:2 * HEAD_DIM + 1])
    o_ref[...] = jnp.concatenate(outs, axis=0).T.astype(o_ref.dtype)


def _na_row_tables():
    e = np.arange(NA_R)
    jrow = np.arange(NA_WIN)
    half = NA_ROWS // 2
    lo = np.stack([np.maximum(e - half, 0), e, half + np.minimum(e, half)])
    valid = (jrow[None, :, None] >= lo[:, None, :]) & (jrow[None, :, None] < lo[:, None, :] + NA_ROWS)
    kx = np.zeros((3, NA_WIN, GRID_W, LANES), np.float32)
    kx[..., :NA_R] = np.where(valid, 0.0, -NA_BIG)[:, :, None, :]
    qx = np.zeros((NA_R, GRID_W, LANES), np.float32)
    qx[e, :, e] = 1.0
    return (jnp.asarray(kx.reshape(3, NA_WIN * GRID_W, LANES), BF16),
            jnp.asarray(qx.reshape(NA_R * GRID_W, LANES), BF16))


def _na_attn(za, g, B, L):
    T = B * L
    rows = L // GRID_W
    assert rows >= NA_WIN and rows % NA_R == 0
    nr = rows // NA_R
    nq = NA_R * GRID_W
    nk = NA_WIN * GRID_W
    npair = N_HEADS_A // 2
    kx, qx = _na_row_tables()
    return pl.pallas_call(
        functools.partial(_na_kernel, rows=rows),
        grid=(B, npair, nr),
        in_specs=[
            pl.BlockSpec((nq, LANES), lambda b, hp, i: (b * nr + i, hp)),
            pl.BlockSpec((L, LANES), lambda b, hp, i: (b, npair + hp)),
            pl.BlockSpec((L, LANES), lambda b, hp, i: (b, 2 * npair + hp)),
            pl.BlockSpec((2, g.shape[1], nq), lambda b, hp, i: (hp, 0, 0)),
            pl.BlockSpec((None, nk, LANES),
                         lambda b, hp, i: (jnp.where(i == 0, 0, jnp.where(i == nr - 1, 2, 1)), 0, 0)),
            pl.BlockSpec((nq, LANES), lambda b, hp, i: (0, 0)),
        ],
        out_specs=pl.BlockSpec((nq, LANES), lambda b, hp, i: (b * nr + i, hp)),
        out_shape=jax.ShapeDtypeStruct((T, N_HEADS_A * HEAD_DIM), BF16),
        compiler_params=_cparams(("parallel", "parallel", "arbitrary")),
        name="na_attn",
    )(za, za, za, g, kx, qx)


def _na_bias_table(rpb):
    col = np.arange(GRID_W)
    cs = np.clip(col - NA_COLS // 2, 0, GRID_W - NA_COLS)
    kc = np.arange(GRID_W)
    valid = (kc[:, None] >= cs[None, :]) & (kc[:, None] < cs[None, :] + NA_COLS)
    dcol = kc[:, None] - col[None, :] + (NA_COLS - 1)
    sel = ((dcol[None] == np.arange(2 * NA_COLS - 1)[:, None, None]) & valid[None]).astype(np.float32)
    band = jnp.einsum('hrd,dkc->hrkc', rpb.astype(F32), jnp.asarray(sel),
                      precision=lax.Precision.HIGHEST) * LOG2E
    band = jnp.where(valid[None, None], band, NEG_BIG)
    band = jnp.pad(band, ((0, 0), (NA_R, NA_R), (0, 0), (0, 0)))
    njj = NA_WIN + NA_R
    t = jnp.stack([band[:, NA_ROWS - 1 - qr:NA_ROWS - 1 - qr + njj] for qr in range(NA_R)], axis=1)
    t = t.transpose(0, 2, 3, 1, 4)
    return t.reshape(N_HEADS_A, njj * GRID_W, NA_R * GRID_W).astype(F32)


def _conv_kernel(z_ref, zp_ref, zn_ref, w_ref, b_ref, lg_ref, lb_ref, o_ref, u_scr):
    i = pl.program_id(1)
    nt = pl.num_programs(1)
    T = z_ref.shape[0]

    def glu(z):
        return z[:, :C_WIDTH] * jax.nn.sigmoid(z[:, C_WIDTH:])

    u_scr[CONV_HALO:CONV_HALO + T, :] = glu(z_ref[...])
    u_scr[:CONV_HALO, :] = jnp.where(i == 0, 0.0, glu(zp_ref[...]))
    u_scr[CONV_HALO + T:, :] = jnp.where(i == nt - 1, 0.0, glu(zn_ref[...]))

    off = CONV_HALO - CONV_K // 2
    for c in range(T // CONV_CHUNK):
        acc = jnp.zeros((CONV_CHUNK, C_WIDTH), F32)
        for k in range(CONV_K):
            acc = acc + w_ref[k:k + 1, :] * u_scr[pl.ds(c * CONV_CHUNK + off + k, CONV_CHUNK), :]
        y = acc + b_ref[...]
        mu = jnp.mean(y, axis=-1, keepdims=True)
        yc = y - mu
        var = jnp.mean(yc * yc, axis=-1, keepdims=True)
        y = yc * lax.rsqrt(var + LN_EPS) * lg_ref[...] + lb_ref[...]
        o_ref[c * CONV_CHUNK:(c + 1) * CONV_CHUNK, :] = _silu(y).astype(o_ref.dtype)


def _conv_branch(zc, w, b, lg, lb, B, L):
    T = B * L
    nt = L // CONV_T
    hb = CONV_T // CONV_HALO
    last = T // CONV_HALO - 1
    row = pl.BlockSpec((1, C_WIDTH), lambda bb, i: (0, 0))
    return pl.pallas_call(
        _conv_kernel,
        grid=(B, nt),
        in_specs=[
            pl.BlockSpec((CONV_T, 2 * C_WIDTH), lambda bb, i: (bb * nt + i, 0)),
            pl.BlockSpec((CONV_HALO, 2 * C_WIDTH),
                         lambda bb, i: (jnp.maximum((bb * nt + i) * hb - 1, 0), 0)),
            pl.BlockSpec((CONV_HALO, 2 * C_WIDTH),
                         lambda bb, i: (jnp.minimum((bb * nt + i + 1) * hb, last), 0)),
            pl.BlockSpec((32, C_WIDTH), lambda bb, i: (0, 0)),
            row, row, row,
        ],
        out_specs=pl.BlockSpec((CONV_T, C_WIDTH), lambda bb, i: (bb * nt + i, 0)),
        out_shape=jax.ShapeDtypeStruct((T, C_WIDTH), BF16),
        scratch_shapes=[pltpu.VMEM((CONV_T + 2 * CONV_HALO, C_WIDTH), F32)],
        compiler_params=_cparams(("parallel", "arbitrary")),
        name="conv_branch",
    )(zc, zc, zc, w, b, lg, lb)


def _vt_prep_kernel(v_ref, vt_ref):
    v = v_ref[...].astype(F32)
    pad = _ones_row_pad(v.shape[0])
    for h in range(N_HEADS_D):
        vt = v[:, h * D_V_DIFF:(h + 1) * D_V_DIFF].T
        vt_ref[h] = jnp.concatenate([vt, pad], axis=0).astype(BF16)


def _vt_prep(zq):
    T = zq.shape[0]
    w = N_HEADS_D * D_V_DIFF
    dv = D_V_DIFF + VPAD
    return pl.pallas_call(
        _vt_prep_kernel,
        grid=(T // TK,),
        in_specs=[pl.BlockSpec((TK, w), lambda i: (i, 2))],
        out_specs=pl.BlockSpec((N_HEADS_D, None, dv, TK), lambda i: (0, i, 0, 0)),
        out_shape=jax.ShapeDtypeStruct((N_HEADS_D, T // TK, dv, TK), BF16),
        compiler_params=_cparams(("parallel",)),
        name="vt_prep",
    )(zq)


def _diff_attn_kernel(slope_ref, q_ref, k_ref, vt_ref, lq1_ref, lk1_ref, lq2_ref, lk2_ref, sg_ref,
                      o_ref, qs_ref, ek_ref, dbias_ref, s_ref, mc_ref, m_ref, acc_ref, *, lam_init):
    h = pl.program_id(1)
    i = pl.program_id(2)
    tq = q_ref.shape[0]
    nkv = vt_ref.shape[0]
    assert tq == TK
    slope = slope_ref[h] * LOG2E

    lane = lax.broadcasted_iota(jnp.int32, (tq, LANES), 1)
    off = lax.broadcasted_iota(jnp.int32, (tq, LANES), 0).astype(F32) * slope
    hi = off.astype(BF16).astype(F32)
    mid = (off - hi).astype(BF16).astype(F32)
    lo = (off - hi - mid).astype(BF16).astype(F32)
    ek_ref[...] = jnp.where(lane == 0, hi, jnp.where(lane == 1, mid, jnp.where(lane == 2, lo,
                            jnp.where(lane < 6, 1.0, 0.0)))).astype(BF16)
    qx_before = jnp.where(lane < 3, 1.0, jnp.where(lane == 3, -hi, jnp.where(lane == 4, -mid,
                          jnp.where(lane == 5, -lo, 0.0))))
    q = q_ref[...]
    zero = jnp.zeros_like(q)
    qlane = lax.broadcasted_iota(jnp.int32, q.shape, 1)
    q0 = jnp.where(qlane < HEAD_DIM, q, zero)
    q1 = jnp.where(qlane >= HEAD_DIM, q, zero)
    for v, qx in enumerate((zero, qx_before.astype(BF16), (-qx_before).astype(BF16))):
        qs_ref[v, :tq, :LANES] = q0
        qs_ref[v, tq:, :LANES] = q1
        qs_ref[v, :tq, LANES:] = qx
        qs_ref[v, tq:, LANES:] = qx
    rel = (lax.broadcasted_iota(jnp.int32, (TK, tq), 1) - lax.broadcasted_iota(jnp.int32, (TK, tq), 0))
    dbias_ref[...] = -slope * jnp.abs(rel).astype(F32)

    def tile(t):
        return jnp.where(t == 0, i, t - 1 + (t - 1 >= i).astype(jnp.int32))

    def offset(t):
        return -slope * (jnp.abs(i - tile(t)) * TK).astype(F32)

    def scores(t, first):
        j = tile(t)
        kt = k_ref[pl.ds(pl.multiple_of(j * TK, TK), TK), :]
        kaug = jnp.concatenate([kt, ek_ref[...]], axis=1)
        if first:
            d = dbias_ref[...]
            return _dot_nt(kaug, qs_ref[0]) + jnp.concatenate([d, d], axis=1)
        return _dot_nt(kaug, qs_ref[jnp.where(j < i, 1, 2)])

    _flash_loop(nkv, scores, tile, offset, vt_ref, s_ref, mc_ref, m_ref, acc_ref)

    lam = (jnp.exp(jnp.sum(lq1_ref[...] * lk1_ref[...], axis=-1, keepdims=True))
           - jnp.exp(jnp.sum(lq2_ref[...] * lk2_ref[...], axis=-1, keepdims=True)) + lam_init)
    acc = acc_ref[...]
    o = acc[:D_V_DIFF] / acc[D_V_DIFF:D_V_DIFF + 1]
    od = (o[:, :tq] - lam * o[:, tq:]).T
    ms = jnp.mean(od * od, axis=-1, keepdims=True)
    od = od * lax.rsqrt(ms + EPS) * sg_ref[...] * (1.0 - lam_init)
    o_ref[...] = od.astype(o_ref.dtype)


def _diff_attn(zq, vt, lam_params, subln_g, B, L, lam_init):
    T = B * L
    tq = TQ_DIFF
    nq = L // tq
    nkv = L // TK
    nqs = 2 * tq
    dv = D_V_DIFF + VPAD
    slopes = jnp.asarray(ALIBI_SLOPES, F32)
    vec = pl.BlockSpec((1, HEAD_DIM), lambda b, h, i: (0, 0))
    return pl.pallas_call(
        functools.partial(_diff_attn_kernel, lam_init=lam_init),
        grid=(B, N_HEADS_D, nq),
        in_specs=[
            pl.BlockSpec(memory_space=pltpu.SMEM),
            pl.BlockSpec((tq, LANES), lambda b, h, i: (b * nq + i, h)),
            pl.BlockSpec((L, LANES), lambda b, h, i: (b, N_HEADS_D + h)),
            pl.BlockSpec((None, nkv, dv, TK), lambda b, h, i: (h, b, 0, 0)),
            vec, vec, vec, vec,
            pl.BlockSpec((1, D_V_DIFF), lambda b, h, i: (0, 0)),
        ],
        out_specs=pl.BlockSpec((tq, LANES), lambda b, h, i: (b * nq + i, h)),
        out_shape=jax.ShapeDtypeStruct((T, N_HEADS_D * D_V_DIFF), BF16),
        scratch_shapes=[pltpu.VMEM((3, nqs, 2 * LANES), BF16), pltpu.VMEM((TK, LANES), BF16),
                        pltpu.VMEM((TK, tq), F32),
                        pltpu.VMEM((2, TK, nqs), F32), pltpu.VMEM((2, 1, nqs), F32),
                        pltpu.VMEM((1, nqs), F32), pltpu.VMEM((dv, nqs), F32)],
        compiler_params=_cparams(("parallel", "parallel", "arbitrary")),
        name="diff_attn",
    )(slopes, zq, zq, vt, *lam_params, subln_g)


def _rope_tables(L):
    t = np.arange(L)
    row = (t // GRID_W).astype(np.float64)
    col = (t % GRID_W).astype(np.float64)
    half = HEAD_DIM // 2
    inv = ROPE_THETA ** (-np.arange(0, half, 2, dtype=np.float64) / half)
    ang = np.concatenate([row[:, None] * inv, col[:, None] * inv], axis=-1)
    cos = np.cos(ang).astype(np.float32)
    sin = np.sin(ang).astype(np.float32)
    cos_h = np.concatenate([cos, cos], axis=-1)
    sin_h = np.concatenate([-sin, sin], axis=-1)
    return (jnp.asarray(np.tile(cos_h, (1, LANES // HEAD_DIM))),
            jnp.asarray(np.tile(sin_h, (1, LANES // HEAD_DIM))))


_DEINT = np.concatenate([np.arange(0, HEAD_DIM, 2), np.arange(1, HEAD_DIM, 2)])


def _prep_weights(p):
    c = SCALE * LOG2E
    w = {}
    wa = N_HEADS_A * HEAD_DIM
    wq = N_HEADS_B * HEAD_DIM
    wk = N_KV_B * HEAD_DIM
    w_ab = p['w_in_ab']
    qcols = (np.arange(N_HEADS_B)[:, None] * HEAD_DIM + _DEINT[None, :]).reshape(-1)
    kcols = (np.arange(N_KV_B)[:, None] * HEAD_DIM + _DEINT[None, :]).reshape(-1)
    w['w_in_ab'] = jnp.concatenate([
        w_ab[:, :, :wa] * c,
        w_ab[:, :, wa:3 * wa],
        w_ab[:, :, 3 * wa:3 * wa + wq][:, :, qcols],
        w_ab[:, :, 3 * wa + wq:3 * wa + wq + wk][:, :, kcols],
        w_ab[:, :, 3 * wa + wq + wk:],
    ], axis=-1).astype(BF16)
    w['gq'] = jnp.tile(p['qnorm_b'][:, _DEINT] * c, (1, N_HEADS_B))[:, None, :]
    w['gk'] = jnp.tile(p['knorm_b'][:, _DEINT], (1, N_KV_B))[:, None, :]
    w_cd = p['w_in_cd']
    wd = N_HEADS_D * 2 * HEAD_DIM
    w['w_in_cd'] = jnp.concatenate([
        w_cd[:, :, :2 * C_WIDTH],
        w_cd[:, :, 2 * C_WIDTH:2 * C_WIDTH + wd] * c,
        w_cd[:, :, 2 * C_WIDTH + wd:],
    ], axis=-1).astype(BF16)
    for name in ('w_out_ab', 'w_out_cd', 'w1', 'w3', 'w2'):
        w[name] = p[name].astype(BF16)
    seg = np.kron(np.eye(N_HEADS_B, dtype=np.float32), np.ones((HEAD_DIM, HEAD_DIM), np.float32))
    w['seg'] = jnp.asarray(seg, BF16)
    w['na_bias'] = [_na_bias_table(p['rpb_a'][j]) for j in range(p['rpb_a'].shape[0])]
    w['conv_w'] = jnp.pad(p['conv_w_c'], ((0, 0), (0, 32 - CONV_K), (0, 0)))
    return w


def _trunk(x3, mod, p, w, rope):
    B, L, _ = x3.shape
    x = x3.reshape(B * L, D_MODEL)
    wa = N_HEADS_A * HEAD_DIM
    for li in range(DEPTH):
        sh1, sc1, g1, sh2, sc2, g2 = [mod[li, :, k] for k in range(6)]
        j = li // 2
        if li % 2 == 0:
            za, zb = _inproj(x, p['norm_mix_g'][li], sc1, sh1, w['w_in_ab'][j], L,
                             (3 * wa, (N_HEADS_B + 2 * N_KV_B) * HEAD_DIM), (BF16, F32))
            oa = _na_attn(za, w['na_bias'][j], B, L)
            q, k, vt = _gqa_prep(zb, rope[0], rope[1], w['gq'][j], w['gk'][j], w['seg'], L)
            ob = _gqa_attn(q, k, vt, B, L)
            x = _outproj(x, oa, ob, g1, w['w_out_ab'][j], L)
        else:
            zc, zq = _inproj(x, p['norm_mix_g'][li], sc1, sh1, w['w_in_cd'][j], L,
                             (2 * C_WIDTH, 3 * N_HEADS_D * D_V_DIFF), (F32, BF16))
            u = _conv_branch(zc, w['conv_w'][j], p['conv_b_c'][j][None], p['conv_ln_g'][j][None],
                             p['conv_ln_b'][j][None], B, L)
            lam_init = 0.8 - 0.6 * math.exp(-0.3 * li)
            lam_params = [p[n][j][None] for n in ('lam_q1', 'lam_k1', 'lam_q2', 'lam_k2')]
            od = _diff_attn(zq, _vt_prep(zq), lam_params, p['subln_g'][j][None], B, L, lam_init)
            x = _outproj(x, u, od, g1, w['w_out_cd'][j], L)
        x = _ffn(x, p['norm_ffn_g'][li], sc2, sh2, g2, w['w1'][li], w['w3'][li], w['w2'][li],
                 p['final_g'], L, final=(li == DEPTH - 1))
    return x.reshape(B, L, D_MODEL)


def kernel(x_prompt, x_sample, c_prompt, c_sample, w_mod, b_mod, norm_mix_g, norm_ffn_g, w_in_ab, rpb_a,
           qnorm_b, knorm_b, w_out_ab, w_in_cd, conv_w_c, conv_b_c, conv_ln_g, conv_ln_b, lam_q1, lam_k1,
           lam_q2, lam_k2, subln_g, w_out_cd, w1, w3, w2, final_g):
    p = dict(norm_mix_g=norm_mix_g, norm_ffn_g=norm_ffn_g, w_in_ab=w_in_ab, rpb_a=rpb_a, qnorm_b=qnorm_b,
             knorm_b=knorm_b, w_out_ab=w_out_ab, w_in_cd=w_in_cd, conv_w_c=conv_w_c, conv_b_c=conv_b_c,
             conv_ln_g=conv_ln_g, conv_ln_b=conv_ln_b, lam_q1=lam_q1, lam_k1=lam_k1, lam_q2=lam_q2,
             lam_k2=lam_k2, subln_g=subln_g, w_out_cd=w_out_cd, w1=w1, w3=w3, w2=w2, final_g=final_g)
    w = _prep_weights(p)
    bp, bs = c_prompt.shape[0], c_sample.shape[0]
    rows = -(-(bp + bs) // 8) * 8
    c_all = jnp.concatenate([c_prompt, c_sample, jnp.zeros((rows - bp - bs, D_MODEL), F32)], axis=0)
    mod = _modulation(c_all, w_mod, b_mod).reshape(DEPTH, rows, 6, 1, D_MODEL)
    y_prompt = _trunk(x_prompt, mod[:, :bp], p, w, _rope_tables(x_prompt.shape[1]))
    y_sample = _trunk(x_sample, mod[:, bp:bp + bs], p, w, _rope_tables(x_sample.shape[1]))
    return (y_prompt, y_sample)
```

```python
import functools
import math

import numpy as np
import jax
import jax.numpy as jnp
from jax import lax
from jax.experimental import pallas as pl
from jax.experimental.pallas import tpu as pltpu

F32 = jnp.float32
BF16 = jnp.bfloat16

D_MODEL = 1024
DEPTH = 4
GRID_W = 64
HEAD_DIM = 64
N_HEADS_A = 8
NA_ROWS = 8
NA_COLS = 16
N_HEADS_B = 8
N_KV_B = 2
GROUP_B = N_HEADS_B // N_KV_B
ROPE_THETA = 10000.0
C_WIDTH = 512
CONV_K = 31
N_HEADS_D = 4
D_V_DIFF = 2 * HEAD_DIM
D_FF = 2816
EPS = 1e-6
LN_EPS = 1e-5
ALIBI_SLOPES = tuple(2.0 ** (-8.0 * (h + 1) / N_HEADS_D) for h in range(N_HEADS_D))
LOG2E = math.log2(math.e)
SCALE = HEAD_DIM ** -0.5
NEG_BIG = -1e30

LANES = 128
SUBLANES = 8
VMEM_LIMIT = 56 * 1024 * 1024

TM = 512
TM_FFN = 1024
TN_FFN = 1408
TK = 512
TQ_GQA = 256
VPAD = 16
TQ_DIFF = TK
NA_R = 8
NA_WIN = 2 * NA_R
NA_BIG = 2.0 ** 100
CONV_T = 256
CONV_HALO = 16
CONV_CHUNK = 64


def _cparams(sem):
    return pltpu.CompilerParams(dimension_semantics=sem, vmem_limit_bytes=VMEM_LIMIT)


def _dot(a, b):
    return jnp.dot(a, b, preferred_element_type=F32)


def _dot_nt(a, b):
    return lax.dot_general(a, b, (((1,), (1,)), ((), ())), preferred_element_type=F32)


def _silu(x):
    return x * jax.nn.sigmoid(x)


def _rms_mod(x, g, sc, sh):
    ms = jnp.mean(x * x, axis=-1, keepdims=True)
    h = x * lax.rsqrt(ms + EPS) * g
    return h * (1.0 + sc) + sh


def _mod_kernel(c_ref, w_ref, b_ref, o_ref):
    cs = _silu(c_ref[...])
    o_ref[...] = _dot(cs.astype(BF16), w_ref[...].astype(BF16)) + b_ref[...]


def _modulation(c_all, w_mod, b_mod):
    R = c_all.shape[0]
    tn = 1536
    return pl.pallas_call(
        _mod_kernel,
        grid=(DEPTH, 6 * D_MODEL // tn),
        in_specs=[
            pl.BlockSpec((R, D_MODEL), lambda l, j: (0, 0)),
            pl.BlockSpec((None, D_MODEL, tn), lambda l, j: (l, 0, j)),
            pl.BlockSpec((None, 1, tn), lambda l, j: (l, 0, j)),
        ],
        out_specs=pl.BlockSpec((None, R, tn), lambda l, j: (l, 0, j)),
        out_shape=jax.ShapeDtypeStruct((DEPTH, R, 6 * D_MODEL), F32),
        compiler_params=_cparams(("arbitrary", "arbitrary")),
        name="modulation",
    )(c_all, w_mod, b_mod.reshape(DEPTH, 1, 6 * D_MODEL))


def _inproj_kernel(x_ref, g_ref, sc_ref, sh_ref, w_ref, *o_refs, splits):
    h = _rms_mod(x_ref[...], g_ref[...], sc_ref[...], sh_ref[...])
    z = _dot(h.astype(BF16), w_ref[...])
    off = 0
    for o_ref, n in zip(o_refs, splits):
        o_ref[...] = z[:, off:off + n].astype(o_ref.dtype)
        off += n


def _inproj(x, g, sc, sh, w, L, splits, dtypes):
    T = x.shape[0]
    N = w.shape[1]
    per_b = L // TM
    vec = pl.BlockSpec((None, 1, D_MODEL), lambda i: (i // per_b, 0, 0))
    return pl.pallas_call(
        functools.partial(_inproj_kernel, splits=splits),
        grid=(T // TM,),
        in_specs=[
            pl.BlockSpec((TM, D_MODEL), lambda i: (i, 0)),
            pl.BlockSpec((1, D_MODEL), lambda i: (0, 0)),
            vec, vec,
            pl.BlockSpec((D_MODEL, N), lambda i: (0, 0)),
        ],
        out_specs=[pl.BlockSpec((TM, n), lambda i: (i, 0)) for n in splits],
        out_shape=[jax.ShapeDtypeStruct((T, n), dt) for n, dt in zip(splits, dtypes)],
        compiler_params=_cparams(("parallel",)),
        name="inproj",
    )(x, g.reshape(1, D_MODEL), sc, sh, w)


def _outproj_kernel(x_ref, a_ref, b_ref, gate_ref, w_ref, o_ref):
    half = a_ref.shape[1]
    y = _dot(a_ref[...], w_ref[:half, :]) + _dot(b_ref[...], w_ref[half:, :])
    o_ref[...] = x_ref[...] + (1.0 + gate_ref[...]) * y


def _outproj(x, a, b, gate, w, L):
    T = x.shape[0]
    per_b = L // TM
    half = a.shape[1]
    return pl.pallas_call(
        _outproj_kernel,
        grid=(T // TM,),
        in_specs=[
            pl.BlockSpec((TM, D_MODEL), lambda i: (i, 0)),
            pl.BlockSpec((TM, half), lambda i: (i, 0)),
            pl.BlockSpec((TM, half), lambda i: (i, 0)),
            pl.BlockSpec((None, 1, D_MODEL), lambda i: (i // per_b, 0, 0)),
            pl.BlockSpec((2 * half, D_MODEL), lambda i: (0, 0)),
        ],
        out_specs=pl.BlockSpec((TM, D_MODEL), lambda i: (i, 0)),
        out_shape=jax.ShapeDtypeStruct((T, D_MODEL), F32),
        compiler_params=_cparams(("parallel",)),
        name="outproj",
    )(x, a, b, gate, w)


def _ffn_kernel(x_ref, g_ref, sc_ref, sh_ref, gate_ref, w1_ref, w3_ref, w2_ref, fg_ref, o_ref,
                h_scr, acc_scr, *, final):
    j = pl.program_id(1)

    @pl.when(j == 0)
    def _():
        h_scr[...] = _rms_mod(x_ref[...], g_ref[...], sc_ref[...], sh_ref[...]).astype(BF16)
        acc_scr[...] = jnp.zeros_like(acc_scr)

    h = h_scr[...]
    t = _silu(_dot(h, w1_ref[...])) * _dot(h, w3_ref[...])
    acc_scr[...] += _dot(t.astype(BF16), w2_ref[...])

    @pl.when(j == pl.num_programs(1) - 1)
    def _():
        y = x_ref[...] + (1.0 + gate_ref[...]) * acc_scr[...]
        if final:
            ms = jnp.mean(y * y, axis=-1, keepdims=True)
            y = y * lax.rsqrt(ms + EPS) * fg_ref[...]
        o_ref[...] = y


def _ffn(x, g, sc, sh, gate, w1, w3, w2, final_g, L, final):
    T = x.shape[0]
    tm = min(TM_FFN, L)
    per_b = L // tm
    vec = pl.BlockSpec((None, 1, D_MODEL), lambda i, j: (i // per_b, 0, 0))
    row = pl.BlockSpec((1, D_MODEL), lambda i, j: (0, 0))
    return pl.pallas_call(
        functools.partial(_ffn_kernel, final=final),
        grid=(T // tm, D_FF // TN_FFN),
        in_specs=[
            pl.BlockSpec((tm, D_MODEL), lambda i, j: (i, 0)),
            row, vec, vec, vec,
            pl.BlockSpec((D_MODEL, TN_FFN), lambda i, j: (0, j)),
            pl.BlockSpec((D_MODEL, TN_FFN), lambda i, j: (0, j)),
            pl.BlockSpec((TN_FFN, D_MODEL), lambda i, j: (j, 0)),
            row,
        ],
        out_specs=pl.BlockSpec((tm, D_MODEL), lambda i, j: (i, 0)),
        out_shape=jax.ShapeDtypeStruct((T, D_MODEL), F32),
        scratch_shapes=[pltpu.VMEM((tm, D_MODEL), BF16), pltpu.VMEM((tm, D_MODEL), F32)],
        compiler_params=_cparams(("parallel", "arbitrary")),
        name="ffn",
    )(x, g.reshape(1, D_MODEL), sc, sh, gate, w1, w3, w2, final_g.reshape(1, D_MODEL))


def _flash_loop(nkv, score_fn, tile_fn, offset_fn, vt_ref, s_ref, mc_ref, m_ref, acc_ref):
    unroll = max([u for u in (4, 8) if nkv % u == 0 and nkv // u >= 3], default=2)
    assert nkv % unroll == 0

    def produce(t, slot, first=False):
        s = score_fn(t, first)
        s_ref[slot] = s
        mc_ref[slot] = jnp.max(s, axis=0, keepdims=True) + offset_fn(t)

    def consume(t, slot):
        m_prev = m_ref[...]
        m_new = jnp.maximum(m_prev, mc_ref[slot])
        alpha = jnp.exp2(m_prev - m_new)
        p = jnp.exp2(s_ref[slot] - (m_new - offset_fn(t))).astype(BF16)
        acc_ref[...] = alpha * acc_ref[...] + _dot(vt_ref[tile_fn(t)], p)
        m_ref[...] = m_new

    m_ref[...] = jnp.full_like(m_ref, NEG_BIG)
    acc_ref[...] = jnp.zeros_like(acc_ref)
    produce(0, 0, first=True)

    def span(t0, n, last):
        for u in range(n):
            if not (last and u == n - 1):
                produce(t0 + u + 1, (u + 1) % 2)
            consume(t0 + u, u % 2)

    def body(tt, carry):
        span(unroll * tt, unroll, False)
        return carry

    lax.fori_loop(0, nkv // unroll - 1, body, 0)
    span(nkv - unroll, unroll, True)


def _seg_mean_sq(x, seg_ref):
    sq = x * x
    hi = sq.astype(BF16)
    lo = (sq - hi.astype(F32)).astype(BF16)
    n = x.shape[1]
    seg = seg_ref[:n, :n]
    return (_dot(hi, seg) + _dot(lo, seg)) * (1.0 / HEAD_DIM)


def _rope(x, cos, sin_signed, first_half):
    n = x.shape[1]
    half = HEAD_DIM // 2
    rot = jnp.where(first_half, pltpu.roll(x, n - half, 1), pltpu.roll(x, half, 1))
    return x * cos + rot * sin_signed


def _gqa_prep_kernel(z_ref, cos_ref, sin_ref, gq_ref, gk_ref, seg_ref, qt_ref, k_ref, vt_ref):
    wq = N_HEADS_B * HEAD_DIM
    wk = N_KV_B * HEAD_DIM
    z = z_ref[...]
    cos2 = cos_ref[...]
    sin2 = sin_ref[...]
    tm = z.shape[0]

    def first_half(width):
        lane = lax.broadcasted_iota(jnp.int32, (tm, width), 1)
        return (lane % HEAD_DIM) < (HEAD_DIM // 2)

    xq = z[:, :wq]
    xq = xq * lax.rsqrt(_seg_mean_sq(xq, seg_ref) + EPS) * gq_ref[...]
    cosq = jnp.concatenate([cos2] * (wq // LANES), axis=1)
    sinq = jnp.concatenate([sin2] * (wq // LANES), axis=1)
    q = _rope(xq, cosq, sinq, first_half(wq))
    for pair in range(N_HEADS_B // 2):
        qt = q[:, pair * LANES:(pair + 1) * LANES].T
        qt_ref[2 * pair] = qt[:HEAD_DIM].astype(BF16)
        qt_ref[2 * pair + 1] = qt[HEAD_DIM:].astype(BF16)

    xk = z[:, wq:wq + wk]
    xk = xk * lax.rsqrt(_seg_mean_sq(xk, seg_ref) + EPS) * gk_ref[...]
    k = _rope(xk, cos2, sin2, first_half(wk))
    for g in range(N_KV_B):
        k_ref[g] = k[:, g * HEAD_DIM:(g + 1) * HEAD_DIM].astype(BF16)

    vt = z[:, wq + wk:].T
    pad = _ones_row_pad(tm)
    for g in range(N_KV_B):
        vt_ref[g] = jnp.concatenate([vt[g * HEAD_DIM:(g + 1) * HEAD_DIM, :], pad], axis=0).astype(BF16)


def _gqa_prep(zb, cos2, sin2, gq, gk, seg, L):
    T = zb.shape[0]
    per_b = L // TK
    wq = N_HEADS_B * HEAD_DIM
    wk = N_KV_B * HEAD_DIM
    return pl.pallas_call(
        _gqa_prep_kernel,
        grid=(T // TK,),
        in_specs=[
            pl.BlockSpec((TK, wq + 2 * wk), lambda i: (i, 0)),
            pl.BlockSpec((TK, LANES), lambda i: (i % per_b, 0)),
            pl.BlockSpec((TK, LANES), lambda i: (i % per_b, 0)),
            pl.BlockSpec((1, wq), lambda i: (0, 0)),
            pl.BlockSpec((1, wk), lambda i: (0, 0)),
            pl.BlockSpec((wq, wq), lambda i: (0, 0)),
        ],
        out_specs=[
            pl.BlockSpec((N_HEADS_B, HEAD_DIM, TK), lambda i: (0, 0, i)),
            pl.BlockSpec((N_KV_B, TK, HEAD_DIM), lambda i: (0, i, 0)),
            pl.BlockSpec((N_KV_B, None, HEAD_DIM + VPAD, TK), lambda i: (0, i, 0, 0)),
        ],
        out_shape=[
            jax.ShapeDtypeStruct((N_HEADS_B, HEAD_DIM, T), BF16),
            jax.ShapeDtypeStruct((N_KV_B, T, HEAD_DIM), BF16),
            jax.ShapeDtypeStruct((N_KV_B, T // TK, HEAD_DIM + VPAD, TK), BF16),
        ],
        compiler_params=_cparams(("parallel",)),
        name="gqa_prep",
    )(zb, cos2, sin2, gq, gk, seg)


def _ones_row_pad(n):
    row = lax.broadcasted_iota(jnp.int32, (VPAD, n), 0)
    return jnp.where(row == 0, 1.0, 0.0).astype(F32)


def _gqa_attn_kernel(qt_ref, k_ref, vt_ref, o_ref, s_ref, mc_ref, m_ref, acc_ref):
    tq = qt_ref.shape[2]
    nkv = vt_ref.shape[0]

    def scores(t, first):
        kt = k_ref[pl.ds(pl.multiple_of(t * TK, TK), TK), :]
        qt = jnp.concatenate([qt_ref[c] for c in range(GROUP_B)], axis=1)
        return _dot(kt, qt)

    _flash_loop(nkv, scores, lambda t: t, lambda t: 0.0, vt_ref, s_ref, mc_ref, m_ref, acc_ref)
    acc = acc_ref[...]
    o = acc[:HEAD_DIM] / acc[HEAD_DIM:HEAD_DIM + 1]
    for pair in range(GROUP_B // 2):
        two = jnp.concatenate([o[:, (2 * pair) * tq:(2 * pair + 1) * tq],
                               o[:, (2 * pair + 1) * tq:(2 * pair + 2) * tq]], axis=0)
        o_ref[:, pair * LANES:(pair + 1) * LANES] = two.T.astype(o_ref.dtype)


def _gqa_attn(q, k, vt, B, L):
    T = B * L
    tq = TQ_GQA
    nq = L // tq
    nkv = L // TK
    nqs = GROUP_B * tq
    dv = HEAD_DIM + VPAD
    return pl.pallas_call(
        _gqa_attn_kernel,
        grid=(B, N_KV_B, nq),
        in_specs=[
            pl.BlockSpec((GROUP_B, HEAD_DIM, tq), lambda b, g, i: (g, 0, b * nq + i)),
            pl.BlockSpec((None, L, HEAD_DIM), lambda b, g, i: (g, b, 0)),
            pl.BlockSpec((None, nkv, dv, TK), lambda b, g, i: (g, b, 0, 0)),
        ],
        out_specs=pl.BlockSpec((tq, GROUP_B * HEAD_DIM), lambda b, g, i: (b * nq + i, g)),
        out_shape=jax.ShapeDtypeStruct((T, N_HEADS_B * HEAD_DIM), BF16),
        scratch_shapes=[pltpu.VMEM((2, TK, nqs), F32), pltpu.VMEM((2, 1, nqs), F32),
                        pltpu.VMEM((1, nqs), F32), pltpu.VMEM((dv, nqs), F32)],
        compiler_params=_cparams(("parallel", "parallel", "arbitrary")),
        name="gqa_attn",
    )(q, k, vt)


def _na_kernel(q_ref, k_ref, v_ref, g_ref, kx_ref, qx_ref, o_ref, *, rows):
    i = pl.program_id(2)
    nk = NA_WIN * GRID_W
    r0 = i * NA_R
    ws = jnp.clip(r0 - NA_ROWS // 2, 0, rows - NA_WIN)
    koff = pl.multiple_of(ws * GRID_W, GRID_W)
    goff = pl.multiple_of((NA_R - (r0 - ws)) * GRID_W, GRID_W)
    kaug = jnp.concatenate([k_ref[pl.ds(koff, nk), :], kx_ref[...]], axis=1)
    vt = jnp.concatenate([v_ref[pl.ds(koff, nk), :].astype(F32).T, _ones_row_pad(nk)], axis=0).astype(BF16)
    q = q_ref[...]
    lane = lax.broadcasted_iota(jnp.int32, q.shape, 1)
    outs = []
    for hh in range(2):
        in_head = (lane >= hh * HEAD_DIM) & (lane < (hh + 1) * HEAD_DIM)
        qaug = jnp.concatenate([jnp.where(in_head, q, jnp.zeros_like(q)), qx_ref[...]], axis=1)
        s = _dot_nt(kaug, qaug) + g_ref[hh, pl.ds(goff, nk), :]
        p = jnp.exp2(s - jnp.max(s, axis=0, keepdims=True)).astype(BF16)
        acc = _dot(vt, p)
        outs.append(acc[hh * HEAD_DIM:(hh + 1) * HEAD_DIM] / acc[2 * HEAD_DIM:2 * HEAD_DIM + 1])
    o_ref[...] = jnp.concatenate(outs, axis=0).T.astype(o_ref.dtype)


def _na_row_tables():
    e = np.arange(NA_R)
    jrow = np.arange(NA_WIN)
    half = NA_ROWS // 2
    lo = np.stack([np.maximum(e - half, 0), e, half + np.minimum(e, half)])
    valid = (jrow[None, :, None] >= lo[:, None, :]) & (jrow[None, :, None] < lo[:, None, :] + NA_ROWS)
    kx = np.zeros((3, NA_WIN, GRID_W, LANES), np.float32)
    kx[..., :NA_R] = np.where(valid, 0.0, -NA_BIG)[:, :, None, :]
    qx = np.zeros((NA_R, GRID_W, LANES), np.float32)
    qx[e, :, e] = 1.0
    return (jnp.asarray(kx.reshape(3, NA_WIN * GRID_W, LANES), BF16),
            jnp.asarray(qx.reshape(NA_R * GRID_W, LANES), BF16))


def _na_attn(za, g, B, L):
    T = B * L
    rows = L // GRID_W
    assert rows >= NA_WIN and rows % NA_R == 0
    nr = rows // NA_R
    nq = NA_R * GRID_W
    nk = NA_WIN * GRID_W
    npair = N_HEADS_A // 2
    kx, qx = _na_row_tables()
    return pl.pallas_call(
        functools.partial(_na_kernel, rows=rows),
        grid=(B, npair, nr),
        in_specs=[
            pl.BlockSpec((nq, LANES), lambda b, hp, i: (b * nr + i, hp)),
            pl.BlockSpec((L, LANES), lambda b, hp, i: (b, npair + hp)),
            pl.BlockSpec((L, LANES), lambda b, hp, i: (b, 2 * npair + hp)),
            pl.BlockSpec((2, g.shape[1], nq), lambda b, hp, i: (hp, 0, 0)),
            pl.BlockSpec((None, nk, LANES),
                         lambda b, hp, i: (jnp.where(i == 0, 0, jnp.where(i == nr - 1, 2, 1)), 0, 0)),
            pl.BlockSpec((nq, LANES), lambda b, hp, i: (0, 0)),
        ],
        out_specs=pl.BlockSpec((nq, LANES), lambda b, hp, i: (b * nr + i, hp)),
        out_shape=jax.ShapeDtypeStruct((T, N_HEADS_A * HEAD_DIM), BF16),
        compiler_params=_cparams(("parallel", "parallel", "arbitrary")),
        name="na_attn",
    )(za, za, za, g, kx, qx)


def _na_bias_table(rpb):
    col = np.arange(GRID_W)
    cs = np.clip(col - NA_COLS // 2, 0, GRID_W - NA_COLS)
    kc = np.arange(GRID_W)
    valid = (kc[:, None] >= cs[None, :]) & (kc[:, None] < cs[None, :] + NA_COLS)
    dcol = kc[:, None] - col[None, :] + (NA_COLS - 1)
    sel = ((dcol[None] == np.arange(2 * NA_COLS - 1)[:, None, None]) & valid[None]).astype(np.float32)
    band = jnp.einsum('hrd,dkc->hrkc', rpb.astype(F32), jnp.asarray(sel),
                      precision=lax.Precision.HIGHEST) * LOG2E
    band = jnp.where(valid[None, None], band, NEG_BIG)
    band = jnp.pad(band, ((0, 0), (NA_R, NA_R), (0, 0), (0, 0)))
    njj = NA_WIN + NA_R
    t = jnp.stack([band[:, NA_ROWS - 1 - qr:NA_ROWS - 1 - qr + njj] for qr in range(NA_R)], axis=1)
    t = t.transpose(0, 2, 3, 1, 4)
    return t.reshape(N_HEADS_A, njj * GRID_W, NA_R * GRID_W).astype(F32)


def _conv_kernel(z_ref, zp_ref, zn_ref, w_ref, b_ref, lg_ref, lb_ref, o_ref, u_scr):
    i = pl.program_id(1)
    nt = pl.num_programs(1)
    T = z_ref.shape[0]

    def glu(z):
        return z[:, :C_WIDTH] * jax.nn.sigmoid(z[:, C_WIDTH:])

    u_scr[CONV_HALO:CONV_HALO + T, :] = glu(z_ref[...])
    u_scr[:CONV_HALO, :] = jnp.where(i == 0, 0.0, glu(zp_ref[...]))
    u_scr[CONV_HALO + T:, :] = jnp.where(i == nt - 1, 0.0, glu(zn_ref[...]))

    off = CONV_HALO - CONV_K // 2
    half_w = C_WIDTH // 2
    for c in range(T // CONV_CHUNK):
        halves = []
        for lh in range(2):
            lanes = slice(lh * half_w, (lh + 1) * half_w)
            acc = None
            for r in range(SUBLANES):
                part = None
                for k in range(CONV_K):
                    if (off + k) % SUBLANES != r:
                        continue
                    start = c * CONV_CHUNK + (off + k) // SUBLANES * SUBLANES
                    term = w_ref[k:k + 1, lanes] * u_scr[start:start + CONV_CHUNK + SUBLANES, lanes]
                    part = term if part is None else part + term
                shifted = part[r:r + CONV_CHUNK]
                acc = shifted if acc is None else acc + shifted
            halves.append(acc)
        y = jnp.concatenate(halves, axis=1) + b_ref[...]
        mu = jnp.mean(y, axis=-1, keepdims=True)
        yc = y - mu
        var = jnp.mean(yc * yc, axis=-1, keepdims=True)
        y = yc * lax.rsqrt(var + LN_EPS) * lg_ref[...] + lb_ref[...]
        o_ref[c * CONV_CHUNK:(c + 1) * CONV_CHUNK, :] = _silu(y).astype(o_ref.dtype)


def _conv_branch(zc, w, b, lg, lb, B, L):
    T = B * L
    nt = L // CONV_T
    hb = CONV_T // CONV_HALO
    last = T // CONV_HALO - 1
    row = pl.BlockSpec((1, C_WIDTH), lambda bb, i: (0, 0))
    return pl.pallas_call(
        _conv_kernel,
        grid=(B, nt),
        in_specs=[
            pl.BlockSpec((CONV_T, 2 * C_WIDTH), lambda bb, i: (bb * nt + i, 0)),
            pl.BlockSpec((CONV_HALO, 2 * C_WIDTH),
                         lambda bb, i: (jnp.maximum((bb * nt + i) * hb - 1, 0), 0)),
            pl.BlockSpec((CONV_HALO, 2 * C_WIDTH),
                         lambda bb, i: (jnp.minimum((bb * nt + i + 1) * hb, last), 0)),
            pl.BlockSpec((32, C_WIDTH), lambda bb, i: (0, 0)),
            row, row, row,
        ],
        out_specs=pl.BlockSpec((CONV_T, C_WIDTH), lambda bb, i: (bb * nt + i, 0)),
        out_shape=jax.ShapeDtypeStruct((T, C_WIDTH), BF16),
        scratch_shapes=[pltpu.VMEM((CONV_T + 2 * CONV_HALO, C_WIDTH), F32)],
        compiler_params=_cparams(("parallel", "arbitrary")),
        name="conv_branch",
    )(zc, zc, zc, w, b, lg, lb)


def _vt_prep_kernel(v_ref, vt_ref):
    v = v_ref[...].astype(F32)
    pad = _ones_row_pad(v.shape[0])
    for h in range(N_HEADS_D):
        vt = v[:, h * D_V_DIFF:(h + 1) * D_V_DIFF].T
        vt_ref[h] = jnp.concatenate([vt, pad], axis=0).astype(BF16)


def _vt_prep(zq):
    T = zq.shape[0]
    w = N_HEADS_D * D_V_DIFF
    dv = D_V_DIFF + VPAD
    return pl.pallas_call(
        _vt_prep_kernel,
        grid=(T // TK,),
        in_specs=[pl.BlockSpec((TK, w), lambda i: (i, 2))],
        out_specs=pl.BlockSpec((N_HEADS_D, None, dv, TK), lambda i: (0, i, 0, 0)),
        out_shape=jax.ShapeDtypeStruct((N_HEADS_D, T // TK, dv, TK), BF16),
        compiler_params=_cparams(("parallel",)),
        name="vt_prep",
    )(zq)


def _diff_attn_kernel(slope_ref, q_ref, k_ref, vt_ref, lq1_ref, lk1_ref, lq2_ref, lk2_ref, sg_ref,
                      o_ref, qs_ref, ek_ref, dbias_ref, s_ref, mc_ref, m_ref, acc_ref, *, lam_init):
    h = pl.program_id(1)
    i = pl.program_id(2)
    tq = q_ref.shape[0]
    nkv = vt_ref.shape[0]
    assert tq == TK
    slope = slope_ref[h] * LOG2E

    @pl.when(i == 0)
    def _():
        lane = lax.broadcasted_iota(jnp.int32, (tq, LANES), 1)
        off = lax.broadcasted_iota(jnp.int32, (tq, LANES), 0).astype(F32) * slope
        hi = off.astype(BF16).astype(F32)
        mid = (off - hi).astype(BF16).astype(F32)
        lo = (off - hi - mid).astype(BF16).astype(F32)
        ek_ref[...] = jnp.where(lane == 0, hi, jnp.where(lane == 1, mid, jnp.where(lane == 2, lo,
                                jnp.where(lane < 6, 1.0, 0.0)))).astype(BF16)
        qx_before = jnp.where(lane < 3, 1.0, jnp.where(lane == 3, -hi, jnp.where(lane == 4, -mid,
                              jnp.where(lane == 5, -lo, 0.0))))
        for v, qx in enumerate((jnp.zeros_like(qx_before), qx_before, -qx_before)):
            qxt = qx.T.astype(BF16)
            qs_ref[v, LANES:, :tq] = qxt
            qs_ref[v, LANES:, tq:] = qxt
        rel = (lax.broadcasted_iota(jnp.int32, (TK, tq), 1) - lax.broadcasted_iota(jnp.int32, (TK, tq), 0))
        dbias_ref[...] = -slope * jnp.abs(rel).astype(F32)

    q = q_ref[...]
    zero = jnp.zeros_like(q)
    qlane = lax.broadcasted_iota(jnp.int32, q.shape, 1)
    q0t = jnp.where(qlane < HEAD_DIM, q, zero).astype(F32).T.astype(BF16)
    q1t = jnp.where(qlane >= HEAD_DIM, q, zero).astype(F32).T.astype(BF16)
    for v in range(3):
        qs_ref[v, :LANES, :tq] = q0t
        qs_ref[v, :LANES, tq:] = q1t

    def tile(t):
        return jnp.where(t == 0, i, t - 1 + (t - 1 >= i).astype(jnp.int32))

    def offset(t):
        return -slope * (jnp.abs(i - tile(t)) * TK).astype(F32)

    def scores(t, first):
        j = tile(t)
        kt = k_ref[pl.ds(pl.multiple_of(j * TK, TK), TK), :]
        kaug = jnp.concatenate([kt, ek_ref[...]], axis=1)
        if first:
            d = dbias_ref[...]
            return _dot(kaug, qs_ref[0]) + jnp.concatenate([d, d], axis=1)
        return _dot(kaug, qs_ref[jnp.where(j < i, 1, 2)])

    _flash_loop(nkv, scores, tile, offset, vt_ref, s_ref, mc_ref, m_ref, acc_ref)

    lam = (jnp.exp(jnp.sum(lq1_ref[...] * lk1_ref[...], axis=-1, keepdims=True))
           - jnp.exp(jnp.sum(lq2_ref[...] * lk2_ref[...], axis=-1, keepdims=True)) + lam_init)
    acc = acc_ref[...]
    o = acc[:D_V_DIFF] / acc[D_V_DIFF:D_V_DIFF + 1]
    od = (o[:, :tq] - lam * o[:, tq:]).T
    ms = jnp.mean(od * od, axis=-1, keepdims=True)
    od = od * lax.rsqrt(ms + EPS) * sg_ref[...] * (1.0 - lam_init)
    o_ref[...] = od.astype(o_ref.dtype)


def _diff_attn(zq, vt, lam_params, subln_g, B, L, lam_init):
    T = B * L
    tq = TQ_DIFF
    nq = L // tq
    nkv = L // TK
    nqs = 2 * tq
    dv = D_V_DIFF + VPAD
    slopes = jnp.asarray(ALIBI_SLOPES, F32)
    vec = pl.BlockSpec((1, HEAD_DIM), lambda b, h, i: (0, 0))
    return pl.pallas_call(
        functools.partial(_diff_attn_kernel, lam_init=lam_init),
        grid=(B, N_HEADS_D, nq),
        in_specs=[
            pl.BlockSpec(memory_space=pltpu.SMEM),
            pl.BlockSpec((tq, LANES), lambda b, h, i: (b * nq + i, h)),
            pl.BlockSpec((L, LANES), lambda b, h, i: (b, N_HEADS_D + h)),
            pl.BlockSpec((None, nkv, dv, TK), lambda b, h, i: (h, b, 0, 0)),
            vec, vec, vec, vec,
            pl.BlockSpec((1, D_V_DIFF), lambda b, h, i: (0, 0)),
        ],
        out_specs=pl.BlockSpec((tq, LANES), lambda b, h, i: (b * nq + i, h)),
        out_shape=jax.ShapeDtypeStruct((T, N_HEADS_D * D_V_DIFF), BF16),
        scratch_shapes=[pltpu.VMEM((3, 2 * LANES, nqs), BF16), pltpu.VMEM((TK, LANES), BF16),
                        pltpu.VMEM((TK, tq), F32),
                        pltpu.VMEM((2, TK, nqs), F32), pltpu.VMEM((2, 1, nqs), F32),
                        pltpu.VMEM((1, nqs), F32), pltpu.VMEM((dv, nqs), F32)],
        compiler_params=_cparams(("parallel", "parallel", "arbitrary")),
        name="diff_attn",
    )(slopes, zq, zq, vt, *lam_params, subln_g)


def _rope_tables(L):
    t = np.arange(L)
    row = (t // GRID_W).astype(np.float64)
    col = (t % GRID_W).astype(np.float64)
    half = HEAD_DIM // 2
    inv = ROPE_THETA ** (-np.arange(0, half, 2, dtype=np.float64) / half)
    ang = np.concatenate([row[:, None] * inv, col[:, None] * inv], axis=-1)
    cos = np.cos(ang).astype(np.float32)
    sin = np.sin(ang).astype(np.float32)
    cos_h = np.concatenate([cos, cos], axis=-1)
    sin_h = np.concatenate([-sin, sin], axis=-1)
    return (jnp.asarray(np.tile(cos_h, (1, LANES // HEAD_DIM))),
            jnp.asarray(np.tile(sin_h, (1, LANES // HEAD_DIM))))


_DEINT = np.concatenate([np.arange(0, HEAD_DIM, 2), np.arange(1, HEAD_DIM, 2)])


def _prep_weights(p):
    c = SCALE * LOG2E
    w = {}
    wa = N_HEADS_A * HEAD_DIM
    wq = N_HEADS_B * HEAD_DIM
    wk = N_KV_B * HEAD_DIM
    w_ab = p['w_in_ab']
    qcols = (np.arange(N_HEADS_B)[:, None] * HEAD_DIM + _DEINT[None, :]).reshape(-1)
    kcols = (np.arange(N_KV_B)[:, None] * HEAD_DIM + _DEINT[None, :]).reshape(-1)
    w['w_in_ab'] = jnp.concatenate([
        w_ab[:, :, :wa] * c,
        w_ab[:, :, wa:3 * wa],
        w_ab[:, :, 3 * wa:3 * wa + wq][:, :, qcols],
        w_ab[:, :, 3 * wa + wq:3 * wa + wq + wk][:, :, kcols],
        w_ab[:, :, 3 * wa + wq + wk:],
    ], axis=-1).astype(BF16)
    w['gq'] = jnp.tile(p['qnorm_b'][:, _DEINT] * c, (1, N_HEADS_B))[:, None, :]
    w['gk'] = jnp.tile(p['knorm_b'][:, _DEINT], (1, N_KV_B))[:, None, :]
    w_cd = p['w_in_cd']
    wd = N_HEADS_D * 2 * HEAD_DIM
    w['w_in_cd'] = jnp.concatenate([
        w_cd[:, :, :2 * C_WIDTH],
        w_cd[:, :, 2 * C_WIDTH:2 * C_WIDTH + wd] * c,
        w_cd[:, :, 2 * C_WIDTH + wd:],
    ], axis=-1).astype(BF16)
    for name in ('w_out_ab', 'w_out_cd', 'w1', 'w3', 'w2'):
        w[name] = p[name].astype(BF16)
    seg = np.kron(np.eye(N_HEADS_B, dtype=np.float32), np.ones((HEAD_DIM, HEAD_DIM), np.float32))
    w['seg'] = jnp.asarray(seg, BF16)
    w['na_bias'] = [_na_bias_table(p['rpb_a'][j]) for j in range(p['rpb_a'].shape[0])]
    w['conv_w'] = jnp.pad(p['conv_w_c'], ((0, 0), (0, 32 - CONV_K), (0, 0)))
    return w


def _trunk(x3, mod, p, w, rope):
    B, L, _ = x3.shape
    x = x3.reshape(B * L, D_MODEL)
    wa = N_HEADS_A * HEAD_DIM
    for li in range(DEPTH):
        sh1, sc1, g1, sh2, sc2, g2 = [mod[li, :, k] for k in range(6)]
        j = li // 2
        if li % 2 == 0:
            za, zb = _inproj(x, p['norm_mix_g'][li], sc1, sh1, w['w_in_ab'][j], L,
                             (3 * wa, (N_HEADS_B + 2 * N_KV_B) * HEAD_DIM), (BF16, F32))
            oa = _na_attn(za, w['na_bias'][j], B, L)
            q, k, vt = _gqa_prep(zb, rope[0], rope[1], w['gq'][j], w['gk'][j], w['seg'], L)
            ob = _gqa_attn(q, k, vt, B, L)
            x = _outproj(x, oa, ob, g1, w['w_out_ab'][j], L)
        else:
            zc, zq = _inproj(x, p['norm_mix_g'][li], sc1, sh1, w['w_in_cd'][j], L,
                             (2 * C_WIDTH, 3 * N_HEADS_D * D_V_DIFF), (F32, BF16))
            u = _conv_branch(zc, w['conv_w'][j], p['conv_b_c'][j][None], p['conv_ln_g'][j][None],
                             p['conv_ln_b'][j][None], B, L)
            lam_init = 0.8 - 0.6 * math.exp(-0.3 * li)
            lam_params = [p[n][j][None] for n in ('lam_q1', 'lam_k1', 'lam_q2', 'lam_k2')]
            od = _diff_attn(zq, _vt_prep(zq), lam_params, p['subln_g'][j][None], B, L, lam_init)
            x = _outproj(x, u, od, g1, w['w_out_cd'][j], L)
        x = _ffn(x, p['norm_ffn_g'][li], sc2, sh2, g2, w['w1'][li], w['w3'][li], w['w2'][li],
                 p['final_g'], L, final=(li == DEPTH - 1))
    return x.reshape(B, L, D_MODEL)


def kernel(x_prompt, x_sample, c_prompt, c_sample, w_mod, b_mod, norm_mix_g, norm_ffn_g, w_in_ab, rpb_a,
           qnorm_b, knorm_b, w_out_ab, w_in_cd, conv_w_c, conv_b_c, conv_ln_g, conv_ln_b, lam_q1, lam_k1,
           lam_q2, lam_k2, subln_g, w_out_cd, w1, w3, w2, final_g):
    p = dict(norm_mix_g=norm_mix_g, norm_ffn_g=norm_ffn_g, w_in_ab=w_in_ab, rpb_a=rpb_a, qnorm_b=qnorm_b,
             knorm_b=knorm_b, w_out_ab=w_out_ab, w_in_cd=w_in_cd, conv_w_c=conv_w_c, conv_b_c=conv_b_c,
             conv_ln_g=conv_ln_g, conv_ln_b=conv_ln_b, lam_q1=lam_q1, lam_k1=lam_k1, lam_q2=lam_q2,
             lam_k2=lam_k2, subln_g=subln_g, w_out_cd=w_out_cd, w1=w1, w3=w3, w2=w2, final_g=final_g)
    w = _prep_weights(p)
    bp, bs = c_prompt.shape[0], c_sample.shape[0]
    rows = -(-(bp + bs) // 8) * 8
    c_all = jnp.concatenate([c_prompt, c_sample, jnp.zeros((rows - bp - bs, D_MODEL), F32)], axis=0)
    mod = _modulation(c_all, w_mod, b_mod).reshape(DEPTH, rows, 6, 1, D_MODEL)
    y_prompt = _trunk(x_prompt, mod[:, :bp], p, w, _rope_tables(x_prompt.shape[1]))
    y_sample = _trunk(x_sample, mod[:, bp:bp + bs], p, w, _rope_tables(x_sample.shape[1]))
    return (y_prompt, y_sample)
```

```python
import functools
import math

import numpy as np
import jax
import jax.numpy as jnp
from jax import lax
from jax.experimental import pallas as pl
from jax.experimental.pallas import tpu as pltpu

F32 = jnp.float32
BF16 = jnp.bfloat16

D_MODEL = 1024
DEPTH = 4
GRID_W = 64
HEAD_DIM = 64
N_HEADS_A = 8
NA_ROWS = 8
NA_COLS = 16
N_HEADS_B = 8
N_KV_B = 2
GROUP_B = N_HEADS_B // N_KV_B
ROPE_THETA = 10000.0
C_WIDTH = 512
CONV_K = 31
N_HEADS_D = 4
D_V_DIFF = 2 * HEAD_DIM
D_FF = 2816
EPS = 1e-6
LN_EPS = 1e-5
ALIBI_SLOPES = tuple(2.0 ** (-8.0 * (h + 1) / N_HEADS_D) for h in range(N_HEADS_D))
LOG2E = math.log2(math.e)
SCALE = HEAD_DIM ** -0.5
NEG_BIG = -1e30

LANES = 128
SUBLANES = 8
VMEM_LIMIT = 56 * 1024 * 1024

TM = 512
MXU_TILE = 256
FF_CHUNKS = ((0, 4 * MXU_TILE), (4 * MXU_TILE, 8 * MXU_TILE), (8 * MXU_TILE, D_FF))
TK = 512
TQ_GQA = 256
VPAD = 16
TQ_DIFF = TK
NA_R = 8
NA_WIN = 2 * NA_R
NA_BIG = 2.0 ** 100
CONV_T = 256
CONV_HALO = 16
CONV_CHUNK = 64


def _cparams(sem):
    return pltpu.CompilerParams(dimension_semantics=sem, vmem_limit_bytes=VMEM_LIMIT)


def _dot(a, b):
    return jnp.dot(a, b, preferred_element_type=F32)


def _dot_nt(a, b):
    return lax.dot_general(a, b, (((1,), (1,)), ((), ())), preferred_element_type=F32)


def _silu(x):
    return x * jax.nn.sigmoid(x)


def _rms_mod(x, g, sc, sh):
    ms = jnp.mean(x * x, axis=-1, keepdims=True)
    h = x * lax.rsqrt(ms + EPS) * g
    return h * (1.0 + sc) + sh


def _mod_kernel(c_ref, w_ref, b_ref, o_ref):
    cs = _silu(c_ref[...])
    o_ref[...] = _dot(cs.astype(BF16), w_ref[...].astype(BF16)) + b_ref[...]


def _modulation(c_all, w_mod, b_mod):
    R = c_all.shape[0]
    tn = 1536
    return pl.pallas_call(
        _mod_kernel,
        grid=(DEPTH, 6 * D_MODEL // tn),
        in_specs=[
            pl.BlockSpec((R, D_MODEL), lambda l, j: (0, 0)),
            pl.BlockSpec((None, D_MODEL, tn), lambda l, j: (l, 0, j)),
            pl.BlockSpec((None, 1, tn), lambda l, j: (l, 0, j)),
        ],
        out_specs=pl.BlockSpec((None, R, tn), lambda l, j: (l, 0, j)),
        out_shape=jax.ShapeDtypeStruct((DEPTH, R, 6 * D_MODEL), F32),
        compiler_params=_cparams(("arbitrary", "arbitrary")),
        name="modulation",
    )(c_all, w_mod, b_mod.reshape(DEPTH, 1, 6 * D_MODEL))


def _inproj_kernel(x_ref, g_ref, sc_ref, sh_ref, w_ref, *o_refs, splits):
    h = _rms_mod(x_ref[...], g_ref[...], sc_ref[...], sh_ref[...])
    z = _dot(h.astype(BF16), w_ref[...])
    off = 0
    for o_ref, n in zip(o_refs, splits):
        o_ref[...] = z[:, off:off + n].astype(o_ref.dtype)
        off += n


def _inproj(x, g, sc, sh, w, L, splits, dtypes):
    T = x.shape[0]
    N = w.shape[1]
    per_b = L // TM
    vec = pl.BlockSpec((None, 1, D_MODEL), lambda i: (i // per_b, 0, 0))
    return pl.pallas_call(
        functools.partial(_inproj_kernel, splits=splits),
        grid=(T // TM,),
        in_specs=[
            pl.BlockSpec((TM, D_MODEL), lambda i: (i, 0)),
            pl.BlockSpec((1, D_MODEL), lambda i: (0, 0)),
            vec, vec,
            pl.BlockSpec((D_MODEL, N), lambda i: (0, 0)),
        ],
        out_specs=[pl.BlockSpec((TM, n), lambda i: (i, 0)) for n in splits],
        out_shape=[jax.ShapeDtypeStruct((T, n), dt) for n, dt in zip(splits, dtypes)],
        compiler_params=_cparams(("parallel",)),
        name="inproj",
    )(x, g.reshape(1, D_MODEL), sc, sh, w)


def _out_ffn_kernel(x_ref, a_ref, b_ref, gate1_ref, wo_ref, g_ref, sc_ref, sh_ref, gate2_ref,
                    w1_ref, w3_ref, w2_ref, fg_ref, o_ref, *, final):
    half = a_ref.shape[1]
    y = _dot(a_ref[...], wo_ref[:half, :]) + _dot(b_ref[...], wo_ref[half:, :])
    x1 = x_ref[...] + (1.0 + gate1_ref[...]) * y
    h = _rms_mod(x1, g_ref[...], sc_ref[...], sh_ref[...]).astype(BF16)
    acc = None
    for lo, hi in FF_CHUNKS:
        t = _silu(_dot(h, w1_ref[:, lo:hi])) * _dot(h, w3_ref[:, lo:hi])
        part = _dot(t.astype(BF16), w2_ref[lo:hi, :])
        acc = part if acc is None else acc + part
    out = x1 + (1.0 + gate2_ref[...]) * acc
    if final:
        ms = jnp.mean(out * out, axis=-1, keepdims=True)
        out = out * lax.rsqrt(ms + EPS) * fg_ref[...]
    o_ref[...] = out


def _out_ffn(x, a, b, gate1, wo, g, sc, sh, gate2, w1, w3, w2, final_g, L, final):
    T = x.shape[0]
    per_b = L // TM
    half = a.shape[1]
    vec = pl.BlockSpec((None, 1, D_MODEL), lambda i: (i // per_b, 0, 0))
    row = pl.BlockSpec((1, D_MODEL), lambda i: (0, 0))

    def resident(shape):
        return pl.BlockSpec(shape, lambda i: (0, 0), pipeline_mode=pl.Buffered(1))

    return pl.pallas_call(
        functools.partial(_out_ffn_kernel, final=final),
        grid=(T // TM,),
        in_specs=[
            pl.BlockSpec((TM, D_MODEL), lambda i: (i, 0)),
            pl.BlockSpec((TM, half), lambda i: (i, 0)),
            pl.BlockSpec((TM, half), lambda i: (i, 0)),
            vec,
            resident((2 * half, D_MODEL)),
            row, vec, vec, vec,
            resident((D_MODEL, D_FF)), resident((D_MODEL, D_FF)), resident((D_FF, D_MODEL)),
            row,
        ],
        out_specs=pl.BlockSpec((TM, D_MODEL), lambda i: (i, 0)),
        out_shape=jax.ShapeDtypeStruct((T, D_MODEL), F32),
        compiler_params=_cparams(("parallel",)),
        name="out_ffn",
    )(x, a, b, gate1, wo, g.reshape(1, D_MODEL), sc, sh, gate2, w1, w3, w2, final_g.reshape(1, D_MODEL))


def _flash_loop(nkv, score_fn, tile_fn, offset_fn, vt_ref, s_ref, mc_ref, m_ref, acc_ref):
    unroll = max([u for u in (4, 8) if nkv % u == 0 and nkv // u >= 3], default=2)
    assert nkv % unroll == 0

    def produce(t, slot, first=False):
        s = score_fn(t, first)
        s_ref[slot] = s
        mc_ref[slot] = jnp.max(s, axis=0, keepdims=True) + offset_fn(t)

    def consume(t, slot):
        m_prev = m_ref[...]
        m_new = jnp.maximum(m_prev, mc_ref[slot])
        alpha = jnp.exp2(m_prev - m_new)
        p = jnp.exp2(s_ref[slot] - (m_new - offset_fn(t))).astype(BF16)
        acc_ref[...] = alpha * acc_ref[...] + _dot(vt_ref[tile_fn(t)], p)
        m_ref[...] = m_new

    m_ref[...] = jnp.full_like(m_ref, NEG_BIG)
    acc_ref[...] = jnp.zeros_like(acc_ref)
    produce(0, 0, first=True)

    def span(t0, n, last):
        for u in range(n):
            if not (last and u == n - 1):
                produce(t0 + u + 1, (u + 1) % 2)
            consume(t0 + u, u % 2)

    def body(tt, carry):
        span(unroll * tt, unroll, False)
        return carry

    lax.fori_loop(0, nkv // unroll - 1, body, 0)
    span(nkv - unroll, unroll, True)


def _seg_mean_sq(x, seg_ref):
    sq = x * x
    hi = sq.astype(BF16)
    lo = (sq - hi.astype(F32)).astype(BF16)
    n = x.shape[1]
    seg = seg_ref[:n, :n]
    return (_dot(hi, seg) + _dot(lo, seg)) * (1.0 / HEAD_DIM)


def _rope(x, cos, sin_signed, first_half):
    n = x.shape[1]
    half = HEAD_DIM // 2
    rot = jnp.where(first_half, pltpu.roll(x, n - half, 1), pltpu.roll(x, half, 1))
    return x * cos + rot * sin_signed


def _gqa_prep_kernel(z_ref, cos_ref, sin_ref, gq_ref, gk_ref, seg_ref, qt_ref, k_ref, vt_ref):
    wq = N_HEADS_B * HEAD_DIM
    wk = N_KV_B * HEAD_DIM
    z = z_ref[...]
    cos2 = cos_ref[...]
    sin2 = sin_ref[...]
    tm = z.shape[0]

    def first_half(width):
        lane = lax.broadcasted_iota(jnp.int32, (tm, width), 1)
        return (lane % HEAD_DIM) < (HEAD_DIM // 2)

    xq = z[:, :wq]
    xq = xq * lax.rsqrt(_seg_mean_sq(xq, seg_ref) + EPS) * gq_ref[...]
    cosq = jnp.concatenate([cos2] * (wq // LANES), axis=1)
    sinq = jnp.concatenate([sin2] * (wq // LANES), axis=1)
    q = _rope(xq, cosq, sinq, first_half(wq))
    for pair in range(N_HEADS_B // 2):
        qt = q[:, pair * LANES:(pair + 1) * LANES].T
        qt_ref[2 * pair] = qt[:HEAD_DIM].astype(BF16)
        qt_ref[2 * pair + 1] = qt[HEAD_DIM:].astype(BF16)

    xk = z[:, wq:wq + wk]
    xk = xk * lax.rsqrt(_seg_mean_sq(xk, seg_ref) + EPS) * gk_ref[...]
    k = _rope(xk, cos2, sin2, first_half(wk))
    for g in range(N_KV_B):
        k_ref[g] = k[:, g * HEAD_DIM:(g + 1) * HEAD_DIM].astype(BF16)

    vt = z[:, wq + wk:].T
    pad = _ones_row_pad(tm)
    for g in range(N_KV_B):
        vt_ref[g] = jnp.concatenate([vt[g * HEAD_DIM:(g + 1) * HEAD_DIM, :], pad], axis=0).astype(BF16)


def _gqa_prep(zb, cos2, sin2, gq, gk, seg, L):
    T = zb.shape[0]
    per_b = L // TK
    wq = N_HEADS_B * HEAD_DIM
    wk = N_KV_B * HEAD_DIM
    return pl.pallas_call(
        _gqa_prep_kernel,
        grid=(T // TK,),
        in_specs=[
            pl.BlockSpec((TK, wq + 2 * wk), lambda i: (i, 0)),
            pl.BlockSpec((TK, LANES), lambda i: (i % per_b, 0)),
            pl.BlockSpec((TK, LANES), lambda i: (i % per_b, 0)),
            pl.BlockSpec((1, wq), lambda i: (0, 0)),
            pl.BlockSpec((1, wk), lambda i: (0, 0)),
            pl.BlockSpec((wq, wq), lambda i: (0, 0)),
        ],
        out_specs=[
            pl.BlockSpec((N_HEADS_B, HEAD_DIM, TK), lambda i: (0, 0, i)),
            pl.BlockSpec((N_KV_B, TK, HEAD_DIM), lambda i: (0, i, 0)),
            pl.BlockSpec((N_KV_B, None, HEAD_DIM + VPAD, TK), lambda i: (0, i, 0, 0)),
        ],
        out_shape=[
            jax.ShapeDtypeStruct((N_HEADS_B, HEAD_DIM, T), BF16),
            jax.ShapeDtypeStruct((N_KV_B, T, HEAD_DIM), BF16),
            jax.ShapeDtypeStruct((N_KV_B, T // TK, HEAD_DIM + VPAD, TK), BF16),
        ],
        compiler_params=_cparams(("parallel",)),
        name="gqa_prep",
    )(zb, cos2, sin2, gq, gk, seg)


def _ones_row_pad(n):
    row = lax.broadcasted_iota(jnp.int32, (VPAD, n), 0)
    return jnp.where(row == 0, 1.0, 0.0).astype(F32)


def _gqa_attn_kernel(qt_ref, k_ref, vt_ref, o_ref, s_ref, mc_ref, m_ref, acc_ref):
    tq = qt_ref.shape[2]
    nkv = vt_ref.shape[0]

    def scores(t, first):
        kt = k_ref[pl.ds(pl.multiple_of(t * TK, TK), TK), :]
        qt = jnp.concatenate([qt_ref[c] for c in range(GROUP_B)], axis=1)
        return _dot(kt, qt)

    _flash_loop(nkv, scores, lambda t: t, lambda t: 0.0, vt_ref, s_ref, mc_ref, m_ref, acc_ref)
    acc = acc_ref[...]
    o = acc[:HEAD_DIM] / acc[HEAD_DIM:HEAD_DIM + 1]
    for pair in range(GROUP_B // 2):
        two = jnp.concatenate([o[:, (2 * pair) * tq:(2 * pair + 1) * tq],
                               o[:, (2 * pair + 1) * tq:(2 * pair + 2) * tq]], axis=0)
        o_ref[:, pair * LANES:(pair + 1) * LANES] = two.T.astype(o_ref.dtype)


def _gqa_attn(q, k, vt, B, L):
    T = B * L
    tq = TQ_GQA
    nq = L // tq
    nkv = L // TK
    nqs = GROUP_B * tq
    dv = HEAD_DIM + VPAD
    return pl.pallas_call(
        _gqa_attn_kernel,
        grid=(B, N_KV_B, nq),
        in_specs=[
            pl.BlockSpec((GROUP_B, HEAD_DIM, tq), lambda b, g, i: (g, 0, b * nq + i)),
            pl.BlockSpec((None, L, HEAD_DIM), lambda b, g, i: (g, b, 0)),
            pl.BlockSpec((None, nkv, dv, TK), lambda b, g, i: (g, b, 0, 0)),
        ],
        out_specs=pl.BlockSpec((tq, GROUP_B * HEAD_DIM), lambda b, g, i: (b * nq + i, g)),
        out_shape=jax.ShapeDtypeStruct((T, N_HEADS_B * HEAD_DIM), BF16),
        scratch_shapes=[pltpu.VMEM((2, TK, nqs), F32), pltpu.VMEM((2, 1, nqs), F32),
                        pltpu.VMEM((1, nqs), F32), pltpu.VMEM((dv, nqs), F32)],
        compiler_params=_cparams(("parallel", "parallel", "arbitrary")),
        name="gqa_attn",
    )(q, k, vt)


def _na_kernel(q_ref, k_ref, v_ref, g_ref, kx_ref, qx_ref, o_ref, *, rows):
    i = pl.program_id(2)
    nk = NA_WIN * GRID_W
    r0 = i * NA_R
    ws = jnp.clip(r0 - NA_ROWS // 2, 0, rows - NA_WIN)
    koff = pl.multiple_of(ws * GRID_W, GRID_W)
    goff = pl.multiple_of((NA_R - (r0 - ws)) * GRID_W, GRID_W)
    kaug = jnp.concatenate([k_ref[pl.ds(koff, nk), :], kx_ref[...]], axis=1)
    vt = jnp.concatenate([v_ref[pl.ds(koff, nk), :].astype(F32).T, _ones_row_pad(nk)], axis=0).astype(BF16)
    q = q_ref[...]
    lane = lax.broadcasted_iota(jnp.int32, q.shape, 1)
    outs = []
    for hh in range(2):
        in_head = (lane >= hh * HEAD_DIM) & (lane < (hh + 1) * HEAD_DIM)
        qaug = jnp.concatenate([jnp.where(in_head, q, jnp.zeros_like(q)), qx_ref[...]], axis=1)
        s = _dot_nt(kaug, qaug) + g_ref[hh, pl.ds(goff, nk), :]
        p = jnp.exp2(s - jnp.max(s, axis=0, keepdims=True)).astype(BF16)
        acc = _dot(vt, p)
        outs.append(acc[hh * HEAD_DIM:(hh + 1) * HEAD_DIM] / acc[2 * HEAD_DIM:2 * HEAD_DIM + 1])
    o_ref[...] = jnp.concatenate(outs, axis=0).T.astype(o_ref.dtype)


def _na_row_tables():
    e = np.arange(NA_R)
    jrow = np.arange(NA_WIN)
    half = NA_ROWS // 2
    lo = np.stack([np.maximum(e - half, 0), e, half + np.minimum(e, half)])
    valid = (jrow[None, :, None] >= lo[:, None, :]) & (jrow[None, :, None] < lo[:, None, :] + NA_ROWS)
    kx = np.zeros((3, NA_WIN, GRID_W, LANES), np.float32)
    kx[..., :NA_R] = np.where(valid, 0.0, -NA_BIG)[:, :, None, :]
    qx = np.zeros((NA_R, GRID_W, LANES), np.float32)
    qx[e, :, e] = 1.0
    return (jnp.asarray(kx.reshape(3, NA_WIN * GRID_W, LANES), BF16),
            jnp.asarray(qx.reshape(NA_R * GRID_W, LANES), BF16))


def _na_attn(za, g, B, L):
    T = B * L
    rows = L // GRID_W
    assert rows >= NA_WIN and rows % NA_R == 0
    nr = rows // NA_R
    nq = NA_R * GRID_W
    nk = NA_WIN * GRID_W
    npair = N_HEADS_A // 2
    kx, qx = _na_row_tables()
    return pl.pallas_call(
        functools.partial(_na_kernel, rows=rows),
        grid=(B, npair, nr),
        in_specs=[
            pl.BlockSpec((nq, LANES), lambda b, hp, i: (b * nr + i, hp)),
            pl.BlockSpec((L, LANES), lambda b, hp, i: (b, npair + hp)),
            pl.BlockSpec((L, LANES), lambda b, hp, i: (b, 2 * npair + hp)),
            pl.BlockSpec((2, g.shape[1], nq), lambda b, hp, i: (hp, 0, 0)),
            pl.BlockSpec((None, nk, LANES),
                         lambda b, hp, i: (jnp.where(i == 0, 0, jnp.where(i == nr - 1, 2, 1)), 0, 0)),
            pl.BlockSpec((nq, LANES), lambda b, hp, i: (0, 0)),
        ],
        out_specs=pl.BlockSpec((nq, LANES), lambda b, hp, i: (b * nr + i, hp)),
        out_shape=jax.ShapeDtypeStruct((T, N_HEADS_A * HEAD_DIM), BF16),
        compiler_params=_cparams(("parallel", "parallel", "arbitrary")),
        name="na_attn",
    )(za, za, za, g, kx, qx)


def _na_bias_table(rpb):
    col = np.arange(GRID_W)
    cs = np.clip(col - NA_COLS // 2, 0, GRID_W - NA_COLS)
    kc = np.arange(GRID_W)
    valid = (kc[:, None] >= cs[None, :]) & (kc[:, None] < cs[None, :] + NA_COLS)
    dcol = kc[:, None] - col[None, :] + (NA_COLS - 1)
    sel = ((dcol[None] == np.arange(2 * NA_COLS - 1)[:, None, None]) & valid[None]).astype(np.float32)
    band = jnp.einsum('hrd,dkc->hrkc', rpb.astype(F32), jnp.asarray(sel),
                      precision=lax.Precision.HIGHEST) * LOG2E
    band = jnp.where(valid[None, None], band, NEG_BIG)
    band = jnp.pad(band, ((0, 0), (NA_R, NA_R), (0, 0), (0, 0)))
    njj = NA_WIN + NA_R
    t = jnp.stack([band[:, NA_ROWS - 1 - qr:NA_ROWS - 1 - qr + njj] for qr in range(NA_R)], axis=1)
    t = t.transpose(0, 2, 3, 1, 4)
    return t.reshape(N_HEADS_A, njj * GRID_W, NA_R * GRID_W).astype(F32)


def _conv_kernel(z_ref, zp_ref, zn_ref, w_ref, b_ref, lg_ref, lb_ref, o_ref, u_scr):
    i = pl.program_id(1)
    nt = pl.num_programs(1)
    T = z_ref.shape[0]

    def glu(z):
        return z[:, :C_WIDTH] * jax.nn.sigmoid(z[:, C_WIDTH:])

    u_scr[CONV_HALO:CONV_HALO + T, :] = glu(z_ref[...])
    u_scr[:CONV_HALO, :] = jnp.where(i == 0, 0.0, glu(zp_ref[...]))
    u_scr[CONV_HALO + T:, :] = jnp.where(i == nt - 1, 0.0, glu(zn_ref[...]))

    off = CONV_HALO - CONV_K // 2
    half_w = C_WIDTH // 2
    for c in range(T // CONV_CHUNK):
        halves = []
        for lh in range(2):
            lanes = slice(lh * half_w, (lh + 1) * half_w)
            acc = None
            for r in range(SUBLANES):
                part = None
                for k in range(CONV_K):
                    if (off + k) % SUBLANES != r:
                        continue
                    start = c * CONV_CHUNK + (off + k) // SUBLANES * SUBLANES
                    term = w_ref[k:k + 1, lanes] * u_scr[start:start + CONV_CHUNK + SUBLANES, lanes]
                    part = term if part is None else part + term
                shifted = part[r:r + CONV_CHUNK]
                acc = shifted if acc is None else acc + shifted
            halves.append(acc)
        y = jnp.concatenate(halves, axis=1) + b_ref[...]
        mu = jnp.mean(y, axis=-1, keepdims=True)
        yc = y - mu
        var = jnp.mean(yc * yc, axis=-1, keepdims=True)
        y = yc * lax.rsqrt(var + LN_EPS) * lg_ref[...] + lb_ref[...]
        o_ref[c * CONV_CHUNK:(c + 1) * CONV_CHUNK, :] = _silu(y).astype(o_ref.dtype)


def _conv_branch(zc, w, b, lg, lb, B, L):
    T = B * L
    nt = L // CONV_T
    hb = CONV_T // CONV_HALO
    last = T // CONV_HALO - 1
    row = pl.BlockSpec((1, C_WIDTH), lambda bb, i: (0, 0))
    return pl.pallas_call(
        _conv_kernel,
        grid=(B, nt),
        in_specs=[
            pl.BlockSpec((CONV_T, 2 * C_WIDTH), lambda bb, i: (bb * nt + i, 0)),
            pl.BlockSpec((CONV_HALO, 2 * C_WIDTH),
                         lambda bb, i: (jnp.maximum((bb * nt + i) * hb - 1, 0), 0)),
            pl.BlockSpec((CONV_HALO, 2 * C_WIDTH),
                         lambda bb, i: (jnp.minimum((bb * nt + i + 1) * hb, last), 0)),
            pl.BlockSpec((32, C_WIDTH), lambda bb, i: (0, 0)),
            row, row, row,
        ],
        out_specs=pl.BlockSpec((CONV_T, C_WIDTH), lambda bb, i: (bb * nt + i, 0)),
        out_shape=jax.ShapeDtypeStruct((T, C_WIDTH), BF16),
        scratch_shapes=[pltpu.VMEM((CONV_T + 2 * CONV_HALO, C_WIDTH), F32)],
        compiler_params=_cparams(("parallel", "arbitrary")),
        name="conv_branch",
    )(zc, zc, zc, w, b, lg, lb)


def _vt_prep_kernel(v_ref, vt_ref):
    v = v_ref[...].astype(F32)
    pad = _ones_row_pad(v.shape[0])
    for h in range(N_HEADS_D):
        vt = v[:, h * D_V_DIFF:(h + 1) * D_V_DIFF].T
        vt_ref[h] = jnp.concatenate([vt, pad], axis=0).astype(BF16)


def _vt_prep(zq):
    T = zq.shape[0]
    w = N_HEADS_D * D_V_DIFF
    dv = D_V_DIFF + VPAD
    return pl.pallas_call(
        _vt_prep_kernel,
        grid=(T // TK,),
        in_specs=[pl.BlockSpec((TK, w), lambda i: (i, 2))],
        out_specs=pl.BlockSpec((N_HEADS_D, None, dv, TK), lambda i: (0, i, 0, 0)),
        out_shape=jax.ShapeDtypeStruct((N_HEADS_D, T // TK, dv, TK), BF16),
        compiler_params=_cparams(("parallel",)),
        name="vt_prep",
    )(zq)


def _diff_attn_kernel(slope_ref, q_ref, k_ref, vt_ref, lq1_ref, lk1_ref, lq2_ref, lk2_ref, sg_ref,
                      o_ref, qs_ref, ek_ref, dbias_ref, s_ref, mc_ref, m_ref, acc_ref, *, lam_init):
    h = pl.program_id(1)
    i = pl.program_id(2)
    tq = q_ref.shape[0]
    nkv = vt_ref.shape[0]
    assert tq == TK
    slope = slope_ref[h] * LOG2E

    @pl.when(i == 0)
    def _():
        lane = lax.broadcasted_iota(jnp.int32, (tq, LANES), 1)
        off = lax.broadcasted_iota(jnp.int32, (tq, LANES), 0).astype(F32) * slope
        hi = off.astype(BF16).astype(F32)
        mid = (off - hi).astype(BF16).astype(F32)
        lo = (off - hi - mid).astype(BF16).astype(F32)
        ek_ref[...] = jnp.where(lane == 0, hi, jnp.where(lane == 1, mid, jnp.where(lane == 2, lo,
                                jnp.where(lane < 6, 1.0, 0.0)))).astype(BF16)
        qx_before = jnp.where(lane < 3, 1.0, jnp.where(lane == 3, -hi, jnp.where(lane == 4, -mid,
                              jnp.where(lane == 5, -lo, 0.0))))
        for v, qx in enumerate((jnp.zeros_like(qx_before), qx_before, -qx_before)):
            qxt = qx.T.astype(BF16)
            qs_ref[v, LANES:, :tq] = qxt
            qs_ref[v, LANES:, tq:] = qxt
        rel = (lax.broadcasted_iota(jnp.int32, (TK, tq), 1) - lax.broadcasted_iota(jnp.int32, (TK, tq), 0))
        dbias_ref[...] = -slope * jnp.abs(rel).astype(F32)

    q = q_ref[...]
    zero = jnp.zeros_like(q)
    qlane = lax.broadcasted_iota(jnp.int32, q.shape, 1)
    q0t = jnp.where(qlane < HEAD_DIM, q, zero).astype(F32).T.astype(BF16)
    q1t = jnp.where(qlane >= HEAD_DIM, q, zero).astype(F32).T.astype(BF16)
    for v in range(3):
        qs_ref[v, :LANES, :tq] = q0t
        qs_ref[v, :LANES, tq:] = q1t

    def tile(t):
        return jnp.where(t == 0, i, t - 1 + (t - 1 >= i).astype(jnp.int32))

    def offset(t):
        return -slope * (jnp.abs(i - tile(t)) * TK).astype(F32)

    def scores(t, first):
        j = tile(t)
        kt = k_ref[pl.ds(pl.multiple_of(j * TK, TK), TK), :]
        kaug = jnp.concatenate([kt, ek_ref[...]], axis=1)
        if first:
            d = dbias_ref[...]
            return _dot(kaug, qs_ref[0]) + jnp.concatenate([d, d], axis=1)
        return _dot(kaug, qs_ref[jnp.where(j < i, 1, 2)])

    _flash_loop(nkv, scores, tile, offset, vt_ref, s_ref, mc_ref, m_ref, acc_ref)

    lam = (jnp.exp(jnp.sum(lq1_ref[...] * lk1_ref[...], axis=-1, keepdims=True))
           - jnp.exp(jnp.sum(lq2_ref[...] * lk2_ref[...], axis=-1, keepdims=True)) + lam_init)
    acc = acc_ref[...]
    o = acc[:D_V_DIFF] / acc[D_V_DIFF:D_V_DIFF + 1]
    od = (o[:, :tq] - lam * o[:, tq:]).T
    ms = jnp.mean(od * od, axis=-1, keepdims=True)
    od = od * lax.rsqrt(ms + EPS) * sg_ref[...] * (1.0 - lam_init)
    o_ref[...] = od.astype(o_ref.dtype)


def _diff_attn(zq, vt, lam_params, subln_g, B, L, lam_init):
    T = B * L
    tq = TQ_DIFF
    nq = L // tq
    nkv = L // TK
    nqs = 2 * tq
    dv = D_V_DIFF + VPAD
    slopes = jnp.asarray(ALIBI_SLOPES, F32)
    vec = pl.BlockSpec((1, HEAD_DIM), lambda b, h, i: (0, 0))
    return pl.pallas_call(
        functools.partial(_diff_attn_kernel, lam_init=lam_init),
        grid=(B, N_HEADS_D, nq),
        in_specs=[
            pl.BlockSpec(memory_space=pltpu.SMEM),
            pl.BlockSpec((tq, LANES), lambda b, h, i: (b * nq + i, h)),
            pl.BlockSpec((L, LANES), lambda b, h, i: (b, N_HEADS_D + h)),
            pl.BlockSpec((None, nkv, dv, TK), lambda b, h, i: (h, b, 0, 0)),
            vec, vec, vec, vec,
            pl.BlockSpec((1, D_V_DIFF), lambda b, h, i: (0, 0)),
        ],
        out_specs=pl.BlockSpec((tq, LANES), lambda b, h, i: (b * nq + i, h)),
        out_shape=jax.ShapeDtypeStruct((T, N_HEADS_D * D_V_DIFF), BF16),
        scratch_shapes=[pltpu.VMEM((3, 2 * LANES, nqs), BF16), pltpu.VMEM((TK, LANES), BF16),
                        pltpu.VMEM((TK, tq), F32),
                        pltpu.VMEM((2, TK, nqs), F32), pltpu.VMEM((2, 1, nqs), F32),
                        pltpu.VMEM((1, nqs), F32), pltpu.VMEM((dv, nqs), F32)],
        compiler_params=_cparams(("parallel", "parallel", "arbitrary")),
        name="diff_attn",
    )(slopes, zq, zq, vt, *lam_params, subln_g)


def _rope_tables(L):
    t = np.arange(L)
    row = (t // GRID_W).astype(np.float64)
    col = (t % GRID_W).astype(np.float64)
    half = HEAD_DIM // 2
    inv = ROPE_THETA ** (-np.arange(0, half, 2, dtype=np.float64) / half)
    ang = np.concatenate([row[:, None] * inv, col[:, None] * inv], axis=-1)
    cos = np.cos(ang).astype(np.float32)
    sin = np.sin(ang).astype(np.float32)
    cos_h = np.concatenate([cos, cos], axis=-1)
    sin_h = np.concatenate([-sin, sin], axis=-1)
    return (jnp.asarray(np.tile(cos_h, (1, LANES // HEAD_DIM))),
            jnp.asarray(np.tile(sin_h, (1, LANES // HEAD_DIM))))


_DEINT = np.concatenate([np.arange(0, HEAD_DIM, 2), np.arange(1, HEAD_DIM, 2)])


def _prep_weights(p):
    c = SCALE * LOG2E
    w = {}
    wa = N_HEADS_A * HEAD_DIM
    wq = N_HEADS_B * HEAD_DIM
    wk = N_KV_B * HEAD_DIM
    w_ab = p['w_in_ab']
    qcols = (np.arange(N_HEADS_B)[:, None] * HEAD_DIM + _DEINT[None, :]).reshape(-1)
    kcols = (np.arange(N_KV_B)[:, None] * HEAD_DIM + _DEINT[None, :]).reshape(-1)
    w['w_in_ab'] = jnp.concatenate([
        w_ab[:, :, :wa] * c,
        w_ab[:, :, wa:3 * wa],
        w_ab[:, :, 3 * wa:3 * wa + wq][:, :, qcols],
        w_ab[:, :, 3 * wa + wq:3 * wa + wq + wk][:, :, kcols],
        w_ab[:, :, 3 * wa + wq + wk:],
    ], axis=-1).astype(BF16)
    w['gq'] = jnp.tile(p['qnorm_b'][:, _DEINT] * c, (1, N_HEADS_B))[:, None, :]
    w['gk'] = jnp.tile(p['knorm_b'][:, _DEINT], (1, N_KV_B))[:, None, :]
    w_cd = p['w_in_cd']
    wd = N_HEADS_D * 2 * HEAD_DIM
    w['w_in_cd'] = jnp.concatenate([
        w_cd[:, :, :2 * C_WIDTH],
        w_cd[:, :, 2 * C_WIDTH:2 * C_WIDTH + wd] * c,
        w_cd[:, :, 2 * C_WIDTH + wd:],
    ], axis=-1).astype(BF16)
    for name in ('w_out_ab', 'w_out_cd', 'w1', 'w3', 'w2'):
        w[name] = p[name].astype(BF16)
    seg = np.kron(np.eye(N_HEADS_B, dtype=np.float32), np.ones((HEAD_DIM, HEAD_DIM), np.float32))
    w['seg'] = jnp.asarray(seg, BF16)
    w['na_bias'] = [_na_bias_table(p['rpb_a'][j]) for j in range(p['rpb_a'].shape[0])]
    w['conv_w'] = jnp.pad(p['conv_w_c'], ((0, 0), (0, 32 - CONV_K), (0, 0)))
    return w


def _trunk(x3, mod, p, w, rope):
    B, L, _ = x3.shape
    x = x3.reshape(B * L, D_MODEL)
    wa = N_HEADS_A * HEAD_DIM
    for li in range(DEPTH):
        sh1, sc1, g1, sh2, sc2, g2 = [mod[li, :, k] for k in range(6)]
        j = li // 2
        if li % 2 == 0:
            za, zb = _inproj(x, p['norm_mix_g'][li], sc1, sh1, w['w_in_ab'][j], L,
                             (3 * wa, (N_HEADS_B + 2 * N_KV_B) * HEAD_DIM), (BF16, F32))
            oa = _na_attn(za, w['na_bias'][j], B, L)
            q, k, vt = _gqa_prep(zb, rope[0], rope[1], w['gq'][j], w['gk'][j], w['seg'], L)
            mix, w_out = (oa, _gqa_attn(q, k, vt, B, L)), w['w_out_ab'][j]
        else:
            zc, zq = _inproj(x, p['norm_mix_g'][li], sc1, sh1, w['w_in_cd'][j], L,
                             (2 * C_WIDTH, 3 * N_HEADS_D * D_V_DIFF), (F32, BF16))
            u = _conv_branch(zc, w['conv_w'][j], p['conv_b_c'][j][None], p['conv_ln_g'][j][None],
                             p['conv_ln_b'][j][None], B, L)
            lam_init = 0.8 - 0.6 * math.exp(-0.3 * li)
            lam_params = [p[n][j][None] for n in ('lam_q1', 'lam_k1', 'lam_q2', 'lam_k2')]
            od = _diff_attn(zq, _vt_prep(zq), lam_params, p['subln_g'][j][None], B, L, lam_init)
            mix, w_out = (u, od), w['w_out_cd'][j]
        x = _out_ffn(x, mix[0], mix[1], g1, w_out, p['norm_ffn_g'][li], sc2, sh2, g2,
                     w['w1'][li], w['w3'][li], w['w2'][li], p['final_g'], L, final=(li == DEPTH - 1))
    return x.reshape(B, L, D_MODEL)


def kernel(x_prompt, x_sample, c_prompt, c_sample, w_mod, b_mod, norm_mix_g, norm_ffn_g, w_in_ab, rpb_a,
           qnorm_b, knorm_b, w_out_ab, w_in_cd, conv_w_c, conv_b_c, conv_ln_g, conv_ln_b, lam_q1, lam_k1,
           lam_q2, lam_k2, subln_g, w_out_cd, w1, w3, w2, final_g):
    p = dict(norm_mix_g=norm_mix_g, norm_ffn_g=norm_ffn_g, w_in_ab=w_in_ab, rpb_a=rpb_a, qnorm_b=qnorm_b,
             knorm_b=knorm_b, w_out_ab=w_out_ab, w_in_cd=w_in_cd, conv_w_c=conv_w_c, conv_b_c=conv_b_c,
             conv_ln_g=conv_ln_g, conv_ln_b=conv_ln_b, lam_q1=lam_q1, lam_k1=lam_k1, lam_q2=lam_q2,
             lam_k2=lam_k2, subln_g=subln_g, w_out_cd=w_out_cd, w1=w1, w3=w3, w2=w2, final_g=final_g)
    w = _prep_weights(p)
    bp, bs = c_prompt.shape[0], c_sample.shape[0]
    rows = -(-(bp + bs) // 8) * 8
    c_all = jnp.concatenate([c_prompt, c_sample, jnp.zeros((rows - bp - bs, D_MODEL), F32)], axis=0)
    mod = _modulation(c_all, w_mod, b_mod).reshape(DEPTH, rows, 6, 1, D_MODEL)
    y_prompt = _trunk(x_prompt, mod[:, :bp], p, w, _rope_tables(x_prompt.shape[1]))
    y_sample = _trunk(x_sample, mod[:, bp:bp + bs], p, w, _rope_tables(x_sample.shape[1]))
    return (y_prompt, y_sample)
```

```python
import functools
import math

import numpy as np
import jax
import jax.numpy as jnp
from jax import lax
from jax.experimental import pallas as pl
from jax.experimental.pallas import tpu as pltpu

F32 = jnp.float32
BF16 = jnp.bfloat16

D_MODEL = 1024
DEPTH = 4
GRID_W = 64
HEAD_DIM = 64
N_HEADS_A = 8
NA_ROWS = 8
NA_COLS = 16
N_HEADS_B = 8
N_KV_B = 2
GROUP_B = N_HEADS_B // N_KV_B
ROPE_THETA = 10000.0
C_WIDTH = 512
CONV_K = 31
N_HEADS_D = 4
D_V_DIFF = 2 * HEAD_DIM
D_FF = 2816
EPS = 1e-6
LN_EPS = 1e-5
ALIBI_SLOPES = tuple(2.0 ** (-8.0 * (h + 1) / N_HEADS_D) for h in range(N_HEADS_D))
LOG2E = math.log2(math.e)
SCALE = HEAD_DIM ** -0.5
NEG_BIG = -1e30

LANES = 128
SUBLANES = 8
VMEM_LIMIT = 56 * 1024 * 1024

TM = 512
MXU_TILE = 256
FF_CHUNKS = ((0, 4 * MXU_TILE), (4 * MXU_TILE, 8 * MXU_TILE), (8 * MXU_TILE, D_FF))
TK = 512
TQ_GQA = 256
VPAD = 16
TQ_DIFF = TK
NA_R = 8
NA_WIN = 2 * NA_R
NA_BIG = 2.0 ** 100
CONV_T = 256
CONV_HALO = 16
CONV_CHUNK = 64


def _cparams(sem):
    return pltpu.CompilerParams(dimension_semantics=sem, vmem_limit_bytes=VMEM_LIMIT)


def _dot(a, b):
    return jnp.dot(a, b, preferred_element_type=F32)


def _dot_nt(a, b):
    return lax.dot_general(a, b, (((1,), (1,)), ((), ())), preferred_element_type=F32)


def _silu(x):
    return x * jax.nn.sigmoid(x)


def _rms_mod(x, g, sc, sh):
    ms = jnp.mean(x * x, axis=-1, keepdims=True)
    h = x * lax.rsqrt(ms + EPS) * g
    return h * (1.0 + sc) + sh


def _mod_kernel(c_ref, w_ref, b_ref, o_ref):
    cs = _silu(c_ref[...])
    o_ref[...] = _dot(cs.astype(BF16), w_ref[...].astype(BF16)) + b_ref[...]


def _modulation(c_all, w_mod, b_mod):
    R = c_all.shape[0]
    tn = 1536
    return pl.pallas_call(
        _mod_kernel,
        grid=(DEPTH, 6 * D_MODEL // tn),
        in_specs=[
            pl.BlockSpec((R, D_MODEL), lambda l, j: (0, 0)),
            pl.BlockSpec((None, D_MODEL, tn), lambda l, j: (l, 0, j)),
            pl.BlockSpec((None, 1, tn), lambda l, j: (l, 0, j)),
        ],
        out_specs=pl.BlockSpec((None, R, tn), lambda l, j: (l, 0, j)),
        out_shape=jax.ShapeDtypeStruct((DEPTH, R, 6 * D_MODEL), F32),
        compiler_params=_cparams(("arbitrary", "arbitrary")),
        name="modulation",
    )(c_all, w_mod, b_mod.reshape(DEPTH, 1, 6 * D_MODEL))


def _inproj_kernel(x_ref, g_ref, sc_ref, sh_ref, w_ref, *o_refs, splits):
    h = _rms_mod(x_ref[...], g_ref[...], sc_ref[...], sh_ref[...])
    z = _dot(h.astype(BF16), w_ref[...])
    off = 0
    for o_ref, n in zip(o_refs, splits):
        o_ref[...] = z[:, off:off + n].astype(o_ref.dtype)
        off += n


def _inproj(x, g, sc, sh, w, L, splits, dtypes):
    T = x.shape[0]
    N = w.shape[1]
    per_b = L // TM
    vec = pl.BlockSpec((None, 1, D_MODEL), lambda i: (i // per_b, 0, 0))
    return pl.pallas_call(
        functools.partial(_inproj_kernel, splits=splits),
        grid=(T // TM,),
        in_specs=[
            pl.BlockSpec((TM, D_MODEL), lambda i: (i, 0)),
            pl.BlockSpec((1, D_MODEL), lambda i: (0, 0)),
            vec, vec,
            pl.BlockSpec((D_MODEL, N), lambda i: (0, 0)),
        ],
        out_specs=[pl.BlockSpec((TM, n), lambda i: (i, 0)) for n in splits],
        out_shape=[jax.ShapeDtypeStruct((T, n), dt) for n, dt in zip(splits, dtypes)],
        compiler_params=_cparams(("parallel",)),
        name="inproj",
    )(x, g.reshape(1, D_MODEL), sc, sh, w)


def _out_ffn_kernel(x_ref, a_ref, b_ref, gate1_ref, wo_ref, g_ref, sc_ref, sh_ref, gate2_ref,
                    w1_ref, w3_ref, w2_ref, fg_ref, o_ref, *, final):
    half = a_ref.shape[1]
    y = _dot(a_ref[...], wo_ref[:half, :]) + _dot(b_ref[...], wo_ref[half:, :])
    x1 = x_ref[...] + (1.0 + gate1_ref[...]) * y
    h = _rms_mod(x1, g_ref[...], sc_ref[...], sh_ref[...]).astype(BF16)
    acc = None
    for lo, hi in FF_CHUNKS:
        t = _silu(_dot(h, w1_ref[:, lo:hi])) * _dot(h, w3_ref[:, lo:hi])
        part = _dot(t.astype(BF16), w2_ref[lo:hi, :])
        acc = part if acc is None else acc + part
    out = x1 + (1.0 + gate2_ref[...]) * acc
    if final:
        ms = jnp.mean(out * out, axis=-1, keepdims=True)
        out = out * lax.rsqrt(ms + EPS) * fg_ref[...]
    o_ref[...] = out


def _out_ffn(x, a, b, gate1, wo, g, sc, sh, gate2, w1, w3, w2, final_g, L, final):
    T = x.shape[0]
    per_b = L // TM
    half = a.shape[1]
    vec = pl.BlockSpec((None, 1, D_MODEL), lambda i: (i // per_b, 0, 0))
    row = pl.BlockSpec((1, D_MODEL), lambda i: (0, 0))

    def resident(shape):
        return pl.BlockSpec(shape, lambda i: (0, 0), pipeline_mode=pl.Buffered(1))

    return pl.pallas_call(
        functools.partial(_out_ffn_kernel, final=final),
        grid=(T // TM,),
        in_specs=[
            pl.BlockSpec((TM, D_MODEL), lambda i: (i, 0)),
            pl.BlockSpec((TM, half), lambda i: (i, 0)),
            pl.BlockSpec((TM, half), lambda i: (i, 0)),
            vec,
            resident((2 * half, D_MODEL)),
            row, vec, vec, vec,
            resident((D_MODEL, D_FF)), resident((D_MODEL, D_FF)), resident((D_FF, D_MODEL)),
            row,
        ],
        out_specs=pl.BlockSpec((TM, D_MODEL), lambda i: (i, 0)),
        out_shape=jax.ShapeDtypeStruct((T, D_MODEL), F32),
        compiler_params=_cparams(("parallel",)),
        name="out_ffn",
    )(x, a, b, gate1, wo, g.reshape(1, D_MODEL), sc, sh, gate2, w1, w3, w2, final_g.reshape(1, D_MODEL))


def _flash_loop(nkv, score_fn, tile_fn, offset_fn, vt_ref, s_ref, mc_ref, m_ref, acc_ref):
    unroll = max([u for u in (4, 8) if nkv % u == 0 and nkv // u >= 3], default=2)
    assert nkv % unroll == 0

    def produce(t, slot, first=False):
        s = score_fn(t, first)
        s_ref[slot] = s
        mc_ref[slot] = jnp.max(s, axis=0, keepdims=True) + offset_fn(t)

    def consume(t, slot):
        m_prev = m_ref[...]
        m_new = jnp.maximum(m_prev, mc_ref[slot])
        alpha = jnp.exp2(m_prev - m_new)
        x = s_ref[slot] - (m_new - offset_fn(t))
        half = x.shape[1] // 2
        p = jnp.concatenate([jnp.exp2(x[:, :half]).astype(BF16), jnp.exp2(x[:, half:].astype(BF16))], axis=1)
        acc_ref[...] = alpha * acc_ref[...] + _dot(vt_ref[tile_fn(t)], p)
        m_ref[...] = m_new

    m_ref[...] = jnp.full_like(m_ref, NEG_BIG)
    acc_ref[...] = jnp.zeros_like(acc_ref)
    produce(0, 0, first=True)

    def span(t0, n, last):
        for u in range(n):
            if not (last and u == n - 1):
                produce(t0 + u + 1, (u + 1) % 2)
            consume(t0 + u, u % 2)

    def body(tt, carry):
        span(unroll * tt, unroll, False)
        return carry

    lax.fori_loop(0, nkv // unroll - 1, body, 0)
    span(nkv - unroll, unroll, True)


def _seg_mean_sq(x, seg_ref):
    sq = x * x
    hi = sq.astype(BF16)
    lo = (sq - hi.astype(F32)).astype(BF16)
    n = x.shape[1]
    seg = seg_ref[:n, :n]
    return (_dot(hi, seg) + _dot(lo, seg)) * (1.0 / HEAD_DIM)


def _rope(x, cos, sin_signed, first_half):
    n = x.shape[1]
    half = HEAD_DIM // 2
    rot = jnp.where(first_half, pltpu.roll(x, n - half, 1), pltpu.roll(x, half, 1))
    return x * cos + rot * sin_signed


def _gqa_prep_kernel(z_ref, cos_ref, sin_ref, gq_ref, gk_ref, seg_ref, qt_ref, k_ref, vt_ref):
    wq = N_HEADS_B * HEAD_DIM
    wk = N_KV_B * HEAD_DIM
    z = z_ref[...]
    cos2 = cos_ref[...]
    sin2 = sin_ref[...]
    tm = z.shape[0]

    def first_half(width):
        lane = lax.broadcasted_iota(jnp.int32, (tm, width), 1)
        return (lane % HEAD_DIM) < (HEAD_DIM // 2)

    xq = z[:, :wq]
    xq = xq * lax.rsqrt(_seg_mean_sq(xq, seg_ref) + EPS) * gq_ref[...]
    cosq = jnp.concatenate([cos2] * (wq // LANES), axis=1)
    sinq = jnp.concatenate([sin2] * (wq // LANES), axis=1)
    q = _rope(xq, cosq, sinq, first_half(wq))
    for pair in range(N_HEADS_B // 2):
        qt = q[:, pair * LANES:(pair + 1) * LANES].T
        qt_ref[2 * pair] = qt[:HEAD_DIM].astype(BF16)
        qt_ref[2 * pair + 1] = qt[HEAD_DIM:].astype(BF16)

    xk = z[:, wq:wq + wk]
    xk = xk * lax.rsqrt(_seg_mean_sq(xk, seg_ref) + EPS) * gk_ref[...]
    k = _rope(xk, cos2, sin2, first_half(wk))
    for g in range(N_KV_B):
        k_ref[g] = k[:, g * HEAD_DIM:(g + 1) * HEAD_DIM].astype(BF16)

    vt = z[:, wq + wk:].T
    pad = _ones_row_pad(tm)
    for g in range(N_KV_B):
        vt_ref[g] = jnp.concatenate([vt[g * HEAD_DIM:(g + 1) * HEAD_DIM, :], pad], axis=0).astype(BF16)


def _gqa_prep(zb, cos2, sin2, gq, gk, seg, L):
    T = zb.shape[0]
    per_b = L // TK
    wq = N_HEADS_B * HEAD_DIM
    wk = N_KV_B * HEAD_DIM
    return pl.pallas_call(
        _gqa_prep_kernel,
        grid=(T // TK,),
        in_specs=[
            pl.BlockSpec((TK, wq + 2 * wk), lambda i: (i, 0)),
            pl.BlockSpec((TK, LANES), lambda i: (i % per_b, 0)),
            pl.BlockSpec((TK, LANES), lambda i: (i % per_b, 0)),
            pl.BlockSpec((1, wq), lambda i: (0, 0)),
            pl.BlockSpec((1, wk), lambda i: (0, 0)),
            pl.BlockSpec((wq, wq), lambda i: (0, 0)),
        ],
        out_specs=[
            pl.BlockSpec((N_HEADS_B, HEAD_DIM, TK), lambda i: (0, 0, i)),
            pl.BlockSpec((N_KV_B, TK, HEAD_DIM), lambda i: (0, i, 0)),
            pl.BlockSpec((N_KV_B, None, HEAD_DIM + VPAD, TK), lambda i: (0, i, 0, 0)),
        ],
        out_shape=[
            jax.ShapeDtypeStruct((N_HEADS_B, HEAD_DIM, T), BF16),
            jax.ShapeDtypeStruct((N_KV_B, T, HEAD_DIM), BF16),
            jax.ShapeDtypeStruct((N_KV_B, T // TK, HEAD_DIM + VPAD, TK), BF16),
        ],
        compiler_params=_cparams(("parallel",)),
        name="gqa_prep",
    )(zb, cos2, sin2, gq, gk, seg)


def _ones_row_pad(n):
    row = lax.broadcasted_iota(jnp.int32, (VPAD, n), 0)
    return jnp.where(row == 0, 1.0, 0.0).astype(F32)


def _gqa_attn_kernel(qt_ref, k_ref, vt_ref, o_ref, s_ref, mc_ref, m_ref, acc_ref):
    tq = qt_ref.shape[2]
    nkv = vt_ref.shape[0]

    def scores(t, first):
        kt = k_ref[pl.ds(pl.multiple_of(t * TK, TK), TK), :]
        qt = jnp.concatenate([qt_ref[c] for c in range(GROUP_B)], axis=1)
        return _dot(kt, qt)

    _flash_loop(nkv, scores, lambda t: t, lambda t: 0.0, vt_ref, s_ref, mc_ref, m_ref, acc_ref)
    acc = acc_ref[...]
    o = acc[:HEAD_DIM] / acc[HEAD_DIM:HEAD_DIM + 1]
    for pair in range(GROUP_B // 2):
        two = jnp.concatenate([o[:, (2 * pair) * tq:(2 * pair + 1) * tq],
                               o[:, (2 * pair + 1) * tq:(2 * pair + 2) * tq]], axis=0)
        o_ref[:, pair * LANES:(pair + 1) * LANES] = two.T.astype(o_ref.dtype)


def _gqa_attn(q, k, vt, B, L):
    T = B * L
    tq = TQ_GQA
    nq = L // tq
    nkv = L // TK
    nqs = GROUP_B * tq
    dv = HEAD_DIM + VPAD
    return pl.pallas_call(
        _gqa_attn_kernel,
        grid=(B, N_KV_B, nq),
        in_specs=[
            pl.BlockSpec((GROUP_B, HEAD_DIM, tq), lambda b, g, i: (g, 0, b * nq + i)),
            pl.BlockSpec((None, L, HEAD_DIM), lambda b, g, i: (g, b, 0)),
            pl.BlockSpec((None, nkv, dv, TK), lambda b, g, i: (g, b, 0, 0)),
        ],
        out_specs=pl.BlockSpec((tq, GROUP_B * HEAD_DIM), lambda b, g, i: (b * nq + i, g)),
        out_shape=jax.ShapeDtypeStruct((T, N_HEADS_B * HEAD_DIM), BF16),
        scratch_shapes=[pltpu.VMEM((2, TK, nqs), F32), pltpu.VMEM((2, 1, nqs), F32),
                        pltpu.VMEM((1, nqs), F32), pltpu.VMEM((dv, nqs), F32)],
        compiler_params=_cparams(("parallel", "parallel", "arbitrary")),
        name="gqa_attn",
    )(q, k, vt)


def _na_kernel(q_ref, k_ref, v_ref, g_ref, kx_ref, qx_ref, o_ref, *, rows):
    i = pl.program_id(2)
    nk = NA_WIN * GRID_W
    r0 = i * NA_R
    ws = jnp.clip(r0 - NA_ROWS // 2, 0, rows - NA_WIN)
    koff = pl.multiple_of(ws * GRID_W, GRID_W)
    goff = pl.multiple_of((NA_R - (r0 - ws)) * GRID_W, GRID_W)
    kaug = jnp.concatenate([k_ref[pl.ds(koff, nk), :], kx_ref[...]], axis=1)
    vt = jnp.concatenate([v_ref[pl.ds(koff, nk), :].astype(F32).T, _ones_row_pad(nk)], axis=0).astype(BF16)
    q = q_ref[...]
    lane = lax.broadcasted_iota(jnp.int32, q.shape, 1)
    outs = []
    for hh in range(2):
        in_head = (lane >= hh * HEAD_DIM) & (lane < (hh + 1) * HEAD_DIM)
        qaug = jnp.concatenate([jnp.where(in_head, q, jnp.zeros_like(q)), qx_ref[...]], axis=1)
        s = _dot_nt(kaug, qaug) + g_ref[hh, pl.ds(goff, nk), :]
        p = jnp.exp2(s - jnp.max(s, axis=0, keepdims=True)).astype(BF16)
        acc = _dot(vt, p)
        outs.append(acc[hh * HEAD_DIM:(hh + 1) * HEAD_DIM] / acc[2 * HEAD_DIM:2 * HEAD_DIM + 1])
    o_ref[...] = jnp.concatenate(outs, axis=0).T.astype(o_ref.dtype)


def _na_row_tables():
    e = np.arange(NA_R)
    jrow = np.arange(NA_WIN)
    half = NA_ROWS // 2
    lo = np.stack([np.maximum(e - half, 0), e, half + np.minimum(e, half)])
    valid = (jrow[None, :, None] >= lo[:, None, :]) & (jrow[None, :, None] < lo[:, None, :] + NA_ROWS)
    kx = np.zeros((3, NA_WIN, GRID_W, LANES), np.float32)
    kx[..., :NA_R] = np.where(valid, 0.0, -NA_BIG)[:, :, None, :]
    qx = np.zeros((NA_R, GRID_W, LANES), np.float32)
    qx[e, :, e] = 1.0
    return (jnp.asarray(kx.reshape(3, NA_WIN * GRID_W, LANES), BF16),
            jnp.asarray(qx.reshape(NA_R * GRID_W, LANES), BF16))


def _na_attn(za, g, B, L):
    T = B * L
    rows = L // GRID_W
    assert rows >= NA_WIN and rows % NA_R == 0
    nr = rows // NA_R
    nq = NA_R * GRID_W
    nk = NA_WIN * GRID_W
    npair = N_HEADS_A // 2
    kx, qx = _na_row_tables()
    return pl.pallas_call(
        functools.partial(_na_kernel, rows=rows),
        grid=(B, npair, nr),
        in_specs=[
            pl.BlockSpec((nq, LANES), lambda b, hp, i: (b * nr + i, hp)),
            pl.BlockSpec((L, LANES), lambda b, hp, i: (b, npair + hp)),
            pl.BlockSpec((L, LANES), lambda b, hp, i: (b, 2 * npair + hp)),
            pl.BlockSpec((2, g.shape[1], nq), lambda b, hp, i: (hp, 0, 0)),
            pl.BlockSpec((None, nk, LANES),
                         lambda b, hp, i: (jnp.where(i == 0, 0, jnp.where(i == nr - 1, 2, 1)), 0, 0)),
            pl.BlockSpec((nq, LANES), lambda b, hp, i: (0, 0)),
        ],
        out_specs=pl.BlockSpec((nq, LANES), lambda b, hp, i: (b * nr + i, hp)),
        out_shape=jax.ShapeDtypeStruct((T, N_HEADS_A * HEAD_DIM), BF16),
        compiler_params=_cparams(("parallel", "parallel", "arbitrary")),
        name="na_attn",
    )(za, za, za, g, kx, qx)


def _na_bias_table(rpb):
    col = np.arange(GRID_W)
    cs = np.clip(col - NA_COLS // 2, 0, GRID_W - NA_COLS)
    kc = np.arange(GRID_W)
    valid = (kc[:, None] >= cs[None, :]) & (kc[:, None] < cs[None, :] + NA_COLS)
    dcol = kc[:, None] - col[None, :] + (NA_COLS - 1)
    sel = ((dcol[None] == np.arange(2 * NA_COLS - 1)[:, None, None]) & valid[None]).astype(np.float32)
    band = jnp.einsum('hrd,dkc->hrkc', rpb.astype(F32), jnp.asarray(sel),
                      precision=lax.Precision.HIGHEST) * LOG2E
    band = jnp.where(valid[None, None], band, NEG_BIG)
    band = jnp.pad(band, ((0, 0), (NA_R, NA_R), (0, 0), (0, 0)))
    njj = NA_WIN + NA_R
    t = jnp.stack([band[:, NA_ROWS - 1 - qr:NA_ROWS - 1 - qr + njj] for qr in range(NA_R)], axis=1)
    t = t.transpose(0, 2, 3, 1, 4)
    return t.reshape(N_HEADS_A, njj * GRID_W, NA_R * GRID_W).astype(F32)


def _conv_kernel(z_ref, zp_ref, zn_ref, w_ref, b_ref, lg_ref, lb_ref, o_ref, u_scr):
    i = pl.program_id(1)
    nt = pl.num_programs(1)
    T = z_ref.shape[0]

    def glu(z):
        return z[:, :C_WIDTH] * jax.nn.sigmoid(z[:, C_WIDTH:])

    u_scr[CONV_HALO:CONV_HALO + T, :] = glu(z_ref[...])
    u_scr[:CONV_HALO, :] = jnp.where(i == 0, 0.0, glu(zp_ref[...]))
    u_scr[CONV_HALO + T:, :] = jnp.where(i == nt - 1, 0.0, glu(zn_ref[...]))

    off = CONV_HALO - CONV_K // 2
    half_w = C_WIDTH // 2
    for c in range(T // CONV_CHUNK):
        halves = []
        for lh in range(2):
            lanes = slice(lh * half_w, (lh + 1) * half_w)
            acc = None
            for r in range(SUBLANES):
                part = None
                for k in range(CONV_K):
                    if (off + k) % SUBLANES != r:
                        continue
                    start = c * CONV_CHUNK + (off + k) // SUBLANES * SUBLANES
                    term = w_ref[k:k + 1, lanes] * u_scr[start:start + CONV_CHUNK + SUBLANES, lanes]
                    part = term if part is None else part + term
                shifted = part[r:r + CONV_CHUNK]
                acc = shifted if acc is None else acc + shifted
            halves.append(acc)
        y = jnp.concatenate(halves, axis=1) + b_ref[...]
        mu = jnp.mean(y, axis=-1, keepdims=True)
        yc = y - mu
        var = jnp.mean(yc * yc, axis=-1, keepdims=True)
        y = yc * lax.rsqrt(var + LN_EPS) * lg_ref[...] + lb_ref[...]
        o_ref[c * CONV_CHUNK:(c + 1) * CONV_CHUNK, :] = _silu(y).astype(o_ref.dtype)


def _conv_branch(zc, w, b, lg, lb, B, L):
    T = B * L
    nt = L // CONV_T
    hb = CONV_T // CONV_HALO
    last = T // CONV_HALO - 1
    row = pl.BlockSpec((1, C_WIDTH), lambda bb, i: (0, 0))
    return pl.pallas_call(
        _conv_kernel,
        grid=(B, nt),
        in_specs=[
            pl.BlockSpec((CONV_T, 2 * C_WIDTH), lambda bb, i: (bb * nt + i, 0)),
            pl.BlockSpec((CONV_HALO, 2 * C_WIDTH),
                         lambda bb, i: (jnp.maximum((bb * nt + i) * hb - 1, 0), 0)),
            pl.BlockSpec((CONV_HALO, 2 * C_WIDTH),
                         lambda bb, i: (jnp.minimum((bb * nt + i + 1) * hb, last), 0)),
            pl.BlockSpec((32, C_WIDTH), lambda bb, i: (0, 0)),
            row, row, row,
        ],
        out_specs=pl.BlockSpec((CONV_T, C_WIDTH), lambda bb, i: (bb * nt + i, 0)),
        out_shape=jax.ShapeDtypeStruct((T, C_WIDTH), BF16),
        scratch_shapes=[pltpu.VMEM((CONV_T + 2 * CONV_HALO, C_WIDTH), F32)],
        compiler_params=_cparams(("parallel", "arbitrary")),
        name="conv_branch",
    )(zc, zc, zc, w, b, lg, lb)


def _vt_prep_kernel(v_ref, vt_ref):
    v = v_ref[...].astype(F32)
    pad = _ones_row_pad(v.shape[0])
    for h in range(N_HEADS_D):
        vt = v[:, h * D_V_DIFF:(h + 1) * D_V_DIFF].T
        vt_ref[h] = jnp.concatenate([vt, pad], axis=0).astype(BF16)


def _vt_prep(zq):
    T = zq.shape[0]
    w = N_HEADS_D * D_V_DIFF
    dv = D_V_DIFF + VPAD
    return pl.pallas_call(
        _vt_prep_kernel,
        grid=(T // TK,),
        in_specs=[pl.BlockSpec((TK, w), lambda i: (i, 2))],
        out_specs=pl.BlockSpec((N_HEADS_D, None, dv, TK), lambda i: (0, i, 0, 0)),
        out_shape=jax.ShapeDtypeStruct((N_HEADS_D, T // TK, dv, TK), BF16),
        compiler_params=_cparams(("parallel",)),
        name="vt_prep",
    )(zq)


def _diff_attn_kernel(slope_ref, q_ref, k_ref, vt_ref, lq1_ref, lk1_ref, lq2_ref, lk2_ref, sg_ref,
                      o_ref, qs_ref, ek_ref, dbias_ref, s_ref, mc_ref, m_ref, acc_ref, *, lam_init):
    h = pl.program_id(1)
    i = pl.program_id(2)
    tq = q_ref.shape[0]
    nkv = vt_ref.shape[0]
    assert tq == TK
    slope = slope_ref[h] * LOG2E

    @pl.when(i == 0)
    def _():
        lane = lax.broadcasted_iota(jnp.int32, (tq, LANES), 1)
        off = lax.broadcasted_iota(jnp.int32, (tq, LANES), 0).astype(F32) * slope
        hi = off.astype(BF16).astype(F32)
        mid = (off - hi).astype(BF16).astype(F32)
        lo = (off - hi - mid).astype(BF16).astype(F32)
        ek_ref[...] = jnp.where(lane == 0, hi, jnp.where(lane == 1, mid, jnp.where(lane == 2, lo,
                                jnp.where(lane < 6, 1.0, 0.0)))).astype(BF16)
        qx_before = jnp.where(lane < 3, 1.0, jnp.where(lane == 3, -hi, jnp.where(lane == 4, -mid,
                              jnp.where(lane == 5, -lo, 0.0))))
        for v, qx in enumerate((jnp.zeros_like(qx_before), qx_before, -qx_before)):
            qxt = qx.T.astype(BF16)
            qs_ref[v, LANES:, :tq] = qxt
            qs_ref[v, LANES:, tq:] = qxt
        rel = (lax.broadcasted_iota(jnp.int32, (TK, tq), 1) - lax.broadcasted_iota(jnp.int32, (TK, tq), 0))
        dbias_ref[...] = -slope * jnp.abs(rel).astype(F32)

    q = q_ref[...]
    zero = jnp.zeros_like(q)
    qlane = lax.broadcasted_iota(jnp.int32, q.shape, 1)
    q0t = jnp.where(qlane < HEAD_DIM, q, zero).astype(F32).T.astype(BF16)
    q1t = jnp.where(qlane >= HEAD_DIM, q, zero).astype(F32).T.astype(BF16)
    for v in range(3):
        qs_ref[v, :LANES, :tq] = q0t
        qs_ref[v, :LANES, tq:] = q1t

    def tile(t):
        return jnp.where(t == 0, i, t - 1 + (t - 1 >= i).astype(jnp.int32))

    def offset(t):
        return -slope * (jnp.abs(i - tile(t)) * TK).astype(F32)

    def scores(t, first):
        j = tile(t)
        kt = k_ref[pl.ds(pl.multiple_of(j * TK, TK), TK), :]
        kaug = jnp.concatenate([kt, ek_ref[...]], axis=1)
        if first:
            d = dbias_ref[...]
            return _dot(kaug, qs_ref[0]) + jnp.concatenate([d, d], axis=1)
        return _dot(kaug, qs_ref[jnp.where(j < i, 1, 2)])

    _flash_loop(nkv, scores, tile, offset, vt_ref, s_ref, mc_ref, m_ref, acc_ref)

    lam = (jnp.exp(jnp.sum(lq1_ref[...] * lk1_ref[...], axis=-1, keepdims=True))
           - jnp.exp(jnp.sum(lq2_ref[...] * lk2_ref[...], axis=-1, keepdims=True)) + lam_init)
    acc = acc_ref[...]
    o = acc[:D_V_DIFF] / acc[D_V_DIFF:D_V_DIFF + 1]
    od = (o[:, :tq] - lam * o[:, tq:]).T
    ms = jnp.mean(od * od, axis=-1, keepdims=True)
    od = od * lax.rsqrt(ms + EPS) * sg_ref[...] * (1.0 - lam_init)
    o_ref[...] = od.astype(o_ref.dtype)


def _diff_attn(zq, vt, lam_params, subln_g, B, L, lam_init):
    T = B * L
    tq = TQ_DIFF
    nq = L // tq
    nkv = L // TK
    nqs = 2 * tq
    dv = D_V_DIFF + VPAD
    slopes = jnp.asarray(ALIBI_SLOPES, F32)
    vec = pl.BlockSpec((1, HEAD_DIM), lambda b, h, i: (0, 0))
    return pl.pallas_call(
        functools.partial(_diff_attn_kernel, lam_init=lam_init),
        grid=(B, N_HEADS_D, nq),
        in_specs=[
            pl.BlockSpec(memory_space=pltpu.SMEM),
            pl.BlockSpec((tq, LANES), lambda b, h, i: (b * nq + i, h)),
            pl.BlockSpec((L, LANES), lambda b, h, i: (b, N_HEADS_D + h)),
            pl.BlockSpec((None, nkv, dv, TK), lambda b, h, i: (h, b, 0, 0)),
            vec, vec, vec, vec,
            pl.BlockSpec((1, D_V_DIFF), lambda b, h, i: (0, 0)),
        ],
        out_specs=pl.BlockSpec((tq, LANES), lambda b, h, i: (b * nq + i, h)),
        out_shape=jax.ShapeDtypeStruct((T, N_HEADS_D * D_V_DIFF), BF16),
        scratch_shapes=[pltpu.VMEM((3, 2 * LANES, nqs), BF16), pltpu.VMEM((TK, LANES), BF16),
                        pltpu.VMEM((TK, tq), F32),
                        pltpu.VMEM((2, TK, nqs), F32), pltpu.VMEM((2, 1, nqs), F32),
                        pltpu.VMEM((1, nqs), F32), pltpu.VMEM((dv, nqs), F32)],
        compiler_params=_cparams(("parallel", "parallel", "arbitrary")),
        name="diff_attn",
    )(slopes, zq, zq, vt, *lam_params, subln_g)


def _rope_tables(L):
    t = np.arange(L)
    row = (t // GRID_W).astype(np.float64)
    col = (t % GRID_W).astype(np.float64)
    half = HEAD_DIM // 2
    inv = ROPE_THETA ** (-np.arange(0, half, 2, dtype=np.float64) / half)
    ang = np.concatenate([row[:, None] * inv, col[:, None] * inv], axis=-1)
    cos = np.cos(ang).astype(np.float32)
    sin = np.sin(ang).astype(np.float32)
    cos_h = np.concatenate([cos, cos], axis=-1)
    sin_h = np.concatenate([-sin, sin], axis=-1)
    return (jnp.asarray(np.tile(cos_h, (1, LANES // HEAD_DIM))),
            jnp.asarray(np.tile(sin_h, (1, LANES // HEAD_DIM))))


_DEINT = np.concatenate([np.arange(0, HEAD_DIM, 2), np.arange(1, HEAD_DIM, 2)])


def _prep_weights(p):
    c = SCALE * LOG2E
    w = {}
    wa = N_HEADS_A * HEAD_DIM
    wq = N_HEADS_B * HEAD_DIM
    wk = N_KV_B * HEAD_DIM
    w_ab = p['w_in_ab']
    qcols = (np.arange(N_HEADS_B)[:, None] * HEAD_DIM + _DEINT[None, :]).reshape(-1)
    kcols = (np.arange(N_KV_B)[:, None] * HEAD_DIM + _DEINT[None, :]).reshape(-1)
    w['w_in_ab'] = jnp.concatenate([
        w_ab[:, :, :wa] * c,
        w_ab[:, :, wa:3 * wa],
        w_ab[:, :, 3 * wa:3 * wa + wq][:, :, qcols],
        w_ab[:, :, 3 * wa + wq:3 * wa + wq + wk][:, :, kcols],
        w_ab[:, :, 3 * wa + wq + wk:],
    ], axis=-1).astype(BF16)
    w['gq'] = jnp.tile(p['qnorm_b'][:, _DEINT] * c, (1, N_HEADS_B))[:, None, :]
    w['gk'] = jnp.tile(p['knorm_b'][:, _DEINT], (1, N_KV_B))[:, None, :]
    w_cd = p['w_in_cd']
    wd = N_HEADS_D * 2 * HEAD_DIM
    w['w_in_cd'] = jnp.concatenate([
        w_cd[:, :, :2 * C_WIDTH],
        w_cd[:, :, 2 * C_WIDTH:2 * C_WIDTH + wd] * c,
        w_cd[:, :, 2 * C_WIDTH + wd:],
    ], axis=-1).astype(BF16)
    for name in ('w_out_ab', 'w_out_cd', 'w1', 'w3', 'w2'):
        w[name] = p[name].astype(BF16)
    seg = np.kron(np.eye(N_HEADS_B, dtype=np.float32), np.ones((HEAD_DIM, HEAD_DIM), np.float32))
    w['seg'] = jnp.asarray(seg, BF16)
    w['na_bias'] = [_na_bias_table(p['rpb_a'][j]) for j in range(p['rpb_a'].shape[0])]
    w['conv_w'] = jnp.pad(p['conv_w_c'], ((0, 0), (0, 32 - CONV_K), (0, 0)))
    return w


def _trunk(x3, mod, p, w, rope):
    B, L, _ = x3.shape
    x = x3.reshape(B * L, D_MODEL)
    wa = N_HEADS_A * HEAD_DIM
    for li in range(DEPTH):
        sh1, sc1, g1, sh2, sc2, g2 = [mod[li, :, k] for k in range(6)]
        j = li // 2
        if li % 2 == 0:
            za, zb = _inproj(x, p['norm_mix_g'][li], sc1, sh1, w['w_in_ab'][j], L,
                             (3 * wa, (N_HEADS_B + 2 * N_KV_B) * HEAD_DIM), (BF16, F32))
            oa = _na_attn(za, w['na_bias'][j], B, L)
            q, k, vt = _gqa_prep(zb, rope[0], rope[1], w['gq'][j], w['gk'][j], w['seg'], L)
            mix, w_out = (oa, _gqa_attn(q, k, vt, B, L)), w['w_out_ab'][j]
        else:
            zc, zq = _inproj(x, p['norm_mix_g'][li], sc1, sh1, w['w_in_cd'][j], L,
                             (2 * C_WIDTH, 3 * N_HEADS_D * D_V_DIFF), (F32, BF16))
            u = _conv_branch(zc, w['conv_w'][j], p['conv_b_c'][j][None], p['conv_ln_g'][j][None],
                             p['conv_ln_b'][j][None], B, L)
            lam_init = 0.8 - 0.6 * math.exp(-0.3 * li)
            lam_params = [p[n][j][None] for n in ('lam_q1', 'lam_k1', 'lam_q2', 'lam_k2')]
            od = _diff_attn(zq, _vt_prep(zq), lam_params, p['subln_g'][j][None], B, L, lam_init)
            mix, w_out = (u, od), w['w_out_cd'][j]
        x = _out_ffn(x, mix[0], mix[1], g1, w_out, p['norm_ffn_g'][li], sc2, sh2, g2,
                     w['w1'][li], w['w3'][li], w['w2'][li], p['final_g'], L, final=(li == DEPTH - 1))
    return x.reshape(B, L, D_MODEL)


def kernel(x_prompt, x_sample, c_prompt, c_sample, w_mod, b_mod, norm_mix_g, norm_ffn_g, w_in_ab, rpb_a,
           qnorm_b, knorm_b, w_out_ab, w_in_cd, conv_w_c, conv_b_c, conv_ln_g, conv_ln_b, lam_q1, lam_k1,
           lam_q2, lam_k2, subln_g, w_out_cd, w1, w3, w2, final_g):
    p = dict(norm_mix_g=norm_mix_g, norm_ffn_g=norm_ffn_g, w_in_ab=w_in_ab, rpb_a=rpb_a, qnorm_b=qnorm_b,
             knorm_b=knorm_b, w_out_ab=w_out_ab, w_in_cd=w_in_cd, conv_w_c=conv_w_c, conv_b_c=conv_b_c,
             conv_ln_g=conv_ln_g, conv_ln_b=conv_ln_b, lam_q1=lam_q1, lam_k1=lam_k1, lam_q2=lam_q2,
             lam_k2=lam_k2, subln_g=subln_g, w_out_cd=w_out_cd, w1=w1, w3=w3, w2=w2, final_g=final_g)
    w = _prep_weights(p)
    bp, bs = c_prompt.shape[0], c_sample.shape[0]
    rows = -(-(bp + bs) // 8) * 8
    c_all = jnp.concatenate([c_prompt, c_sample, jnp.zeros((rows - bp - bs, D_MODEL), F32)], axis=0)
    mod = _modulation(c_all, w_mod, b_mod).reshape(DEPTH, rows, 6, 1, D_MODEL)
    y_prompt = _trunk(x_prompt, mod[:, :bp], p, w, _rope_tables(x_prompt.shape[1]))
    y_sample = _trunk(x_sample, mod[:, bp:bp + bs], p, w, _rope_tables(x_sample.shape[1]))
    return (y_prompt, y_sample)
```

```python
import functools
import math

import numpy as np
import jax
import jax.numpy as jnp
from jax import lax
from jax.experimental import pallas as pl
from jax.experimental.pallas import tpu as pltpu

F32 = jnp.float32
BF16 = jnp.bfloat16

D_MODEL = 1024
DEPTH = 4
GRID_W = 64
HEAD_DIM = 64
N_HEADS_A = 8
NA_ROWS = 8
NA_COLS = 16
N_HEADS_B = 8
N_KV_B = 2
GROUP_B = N_HEADS_B // N_KV_B
ROPE_THETA = 10000.0
C_WIDTH = 512
CONV_K = 31
N_HEADS_D = 4
D_V_DIFF = 2 * HEAD_DIM
D_FF = 2816
EPS = 1e-6
LN_EPS = 1e-5
ALIBI_SLOPES = tuple(2.0 ** (-8.0 * (h + 1) / N_HEADS_D) for h in range(N_HEADS_D))
LOG2E = math.log2(math.e)
SCALE = HEAD_DIM ** -0.5
NEG_BIG = -1e30

LANES = 128
SUBLANES = 8
VMEM_LIMIT = 56 * 1024 * 1024

TM = 512
MXU_TILE = 256
FF_CHUNKS = ((0, 4 * MXU_TILE), (4 * MXU_TILE, 8 * MXU_TILE), (8 * MXU_TILE, D_FF))
TK = 512
TQ_GQA = 256
VPAD = 16
TQ_DIFF = TK
NA_R = 8
NA_WIN = 2 * NA_R
NA_BIG = 2.0 ** 100
CONV_T = 256
CONV_HALO = 16
CONV_CHUNK = 64


def _cparams(sem):
    return pltpu.CompilerParams(dimension_semantics=sem, vmem_limit_bytes=VMEM_LIMIT)


def _dot(a, b):
    return jnp.dot(a, b, preferred_element_type=F32)


def _dot_nt(a, b):
    return lax.dot_general(a, b, (((1,), (1,)), ((), ())), preferred_element_type=F32)


def _silu(x):
    return x * jax.nn.sigmoid(x)


def _rms_mod(x, g, sc, sh):
    ms = jnp.mean(x * x, axis=-1, keepdims=True)
    h = x * lax.rsqrt(ms + EPS) * g
    return h * (1.0 + sc) + sh


def _mod_kernel(c_ref, w_ref, b_ref, o_ref):
    cs = _silu(c_ref[...])
    o_ref[...] = _dot(cs.astype(BF16), w_ref[...].astype(BF16)) + b_ref[...]


def _modulation(c_all, w_mod, b_mod):
    R = c_all.shape[0]
    tn = 1536
    return pl.pallas_call(
        _mod_kernel,
        grid=(DEPTH, 6 * D_MODEL // tn),
        in_specs=[
            pl.BlockSpec((R, D_MODEL), lambda l, j: (0, 0)),
            pl.BlockSpec((None, D_MODEL, tn), lambda l, j: (l, 0, j)),
            pl.BlockSpec((None, 1, tn), lambda l, j: (l, 0, j)),
        ],
        out_specs=pl.BlockSpec((None, R, tn), lambda l, j: (l, 0, j)),
        out_shape=jax.ShapeDtypeStruct((DEPTH, R, 6 * D_MODEL), F32),
        compiler_params=_cparams(("arbitrary", "arbitrary")),
        name="modulation",
    )(c_all, w_mod, b_mod.reshape(DEPTH, 1, 6 * D_MODEL))


def _inproj_kernel(x_ref, g_ref, sc_ref, sh_ref, w_ref, *o_refs, splits):
    h = _rms_mod(x_ref[...], g_ref[...], sc_ref[...], sh_ref[...])
    z = _dot(h.astype(BF16), w_ref[...])
    off = 0
    for o_ref, n in zip(o_refs, splits):
        o_ref[...] = z[:, off:off + n].astype(o_ref.dtype)
        off += n


def _inproj(x, g, sc, sh, w, L, splits, dtypes):
    T = x.shape[0]
    N = w.shape[1]
    per_b = L // TM
    vec = pl.BlockSpec((None, 1, D_MODEL), lambda i: (i // per_b, 0, 0))
    return pl.pallas_call(
        functools.partial(_inproj_kernel, splits=splits),
        grid=(T // TM,),
        in_specs=[
            pl.BlockSpec((TM, D_MODEL), lambda i: (i, 0)),
            pl.BlockSpec((1, D_MODEL), lambda i: (0, 0)),
            vec, vec,
            pl.BlockSpec((D_MODEL, N), lambda i: (0, 0)),
        ],
        out_specs=[pl.BlockSpec((TM, n), lambda i: (i, 0)) for n in splits],
        out_shape=[jax.ShapeDtypeStruct((T, n), dt) for n, dt in zip(splits, dtypes)],
        compiler_params=_cparams(("parallel",)),
        name="inproj",
    )(x, g.reshape(1, D_MODEL), sc, sh, w)


def _out_ffn_kernel(x_ref, a_ref, b_ref, gate1_ref, wo_ref, g_ref, sc_ref, sh_ref, gate2_ref,
                    w1_ref, w3_ref, w2_ref, fg_ref, o_ref, *, final):
    half = a_ref.shape[1]
    y = _dot(a_ref[...], wo_ref[:half, :]) + _dot(b_ref[...], wo_ref[half:, :])
    x1 = x_ref[...] + (1.0 + gate1_ref[...]) * y
    h = _rms_mod(x1, g_ref[...], sc_ref[...], sh_ref[...]).astype(BF16)
    acc = None
    for lo, hi in FF_CHUNKS:
        t = _silu(_dot(h, w1_ref[:, lo:hi])) * _dot(h, w3_ref[:, lo:hi])
        part = _dot(t.astype(BF16), w2_ref[lo:hi, :])
        acc = part if acc is None else acc + part
    out = x1 + (1.0 + gate2_ref[...]) * acc
    if final:
        ms = jnp.mean(out * out, axis=-1, keepdims=True)
        out = out * lax.rsqrt(ms + EPS) * fg_ref[...]
    o_ref[...] = out


def _out_ffn(x, a, b, gate1, wo, g, sc, sh, gate2, w1, w3, w2, final_g, L, final):
    T = x.shape[0]
    per_b = L // TM
    half = a.shape[1]
    vec = pl.BlockSpec((None, 1, D_MODEL), lambda i: (i // per_b, 0, 0))
    row = pl.BlockSpec((1, D_MODEL), lambda i: (0, 0))

    def resident(shape):
        return pl.BlockSpec(shape, lambda i: (0, 0), pipeline_mode=pl.Buffered(1))

    return pl.pallas_call(
        functools.partial(_out_ffn_kernel, final=final),
        grid=(T // TM,),
        in_specs=[
            pl.BlockSpec((TM, D_MODEL), lambda i: (i, 0)),
            pl.BlockSpec((TM, half), lambda i: (i, 0)),
            pl.BlockSpec((TM, half), lambda i: (i, 0)),
            vec,
            resident((2 * half, D_MODEL)),
            row, vec, vec, vec,
            resident((D_MODEL, D_FF)), resident((D_MODEL, D_FF)), resident((D_FF, D_MODEL)),
            row,
        ],
        out_specs=pl.BlockSpec((TM, D_MODEL), lambda i: (i, 0)),
        out_shape=jax.ShapeDtypeStruct((T, D_MODEL), F32),
        compiler_params=_cparams(("parallel",)),
        name="out_ffn",
    )(x, a, b, gate1, wo, g.reshape(1, D_MODEL), sc, sh, gate2, w1, w3, w2, final_g.reshape(1, D_MODEL))


def _flash_loop(nkv, score_fn, tile_fn, offset_fn, vt_ref, s_ref, mc_ref, m_ref, acc_ref):
    unroll = max([u for u in (4, 8) if nkv % u == 0 and nkv // u >= 2], default=2)
    assert nkv % unroll == 0

    def produce(t, slot, first=False):
        s = score_fn(t, first)
        s_ref[slot] = s
        mc_ref[slot] = jnp.max(s, axis=0, keepdims=True) + offset_fn(t)

    def consume(t, slot):
        m_prev = m_ref[...]
        m_new = jnp.maximum(m_prev, mc_ref[slot])
        alpha = jnp.exp2(m_prev - m_new)
        p = jnp.exp2(s_ref[slot] - (m_new - offset_fn(t))).astype(BF16)
        acc_ref[...] = alpha * acc_ref[...] + _dot(vt_ref[tile_fn(t)], p)
        m_ref[...] = m_new

    m_ref[...] = jnp.full_like(m_ref, NEG_BIG)
    acc_ref[...] = jnp.zeros_like(acc_ref)
    produce(0, 0, first=True)

    def span(t0, n, last):
        for u in range(n):
            if not (last and u == n - 1):
                produce(t0 + u + 1, (u + 1) % 2)
            consume(t0 + u, u % 2)

    def body(tt, carry):
        span(unroll * tt, unroll, False)
        return carry

    trips = nkv // unroll - 1
    if trips == 1:
        trips = trips + jnp.minimum(pl.program_id(0), 0)
    lax.fori_loop(0, trips, body, 0)
    span(nkv - unroll, unroll, True)


def _seg_mean_sq(x, seg_ref):
    sq = x * x
    hi = sq.astype(BF16)
    lo = (sq - hi.astype(F32)).astype(BF16)
    n = x.shape[1]
    seg = seg_ref[:n, :n]
    return (_dot(hi, seg) + _dot(lo, seg)) * (1.0 / HEAD_DIM)


def _rope(x, cos, sin_signed, first_half):
    n = x.shape[1]
    half = HEAD_DIM // 2
    rot = jnp.where(first_half, pltpu.roll(x, n - half, 1), pltpu.roll(x, half, 1))
    return x * cos + rot * sin_signed


def _gqa_prep_kernel(z_ref, cos_ref, sin_ref, gq_ref, gk_ref, seg_ref, qt_ref, k_ref, vt_ref):
    wq = N_HEADS_B * HEAD_DIM
    wk = N_KV_B * HEAD_DIM
    z = z_ref[...]
    cos2 = cos_ref[...]
    sin2 = sin_ref[...]
    tm = z.shape[0]

    def first_half(width):
        lane = lax.broadcasted_iota(jnp.int32, (tm, width), 1)
        return (lane % HEAD_DIM) < (HEAD_DIM // 2)

    xq = z[:, :wq]
    xq = xq * lax.rsqrt(_seg_mean_sq(xq, seg_ref) + EPS) * gq_ref[...]
    cosq = jnp.concatenate([cos2] * (wq // LANES), axis=1)
    sinq = jnp.concatenate([sin2] * (wq // LANES), axis=1)
    q = _rope(xq, cosq, sinq, first_half(wq))
    for pair in range(N_HEADS_B // 2):
        qt = q[:, pair * LANES:(pair + 1) * LANES].T
        qt_ref[2 * pair] = qt[:HEAD_DIM].astype(BF16)
        qt_ref[2 * pair + 1] = qt[HEAD_DIM:].astype(BF16)

    xk = z[:, wq:wq + wk]
    xk = xk * lax.rsqrt(_seg_mean_sq(xk, seg_ref) + EPS) * gk_ref[...]
    k = _rope(xk, cos2, sin2, first_half(wk))
    for g in range(N_KV_B):
        k_ref[g] = k[:, g * HEAD_DIM:(g + 1) * HEAD_DIM].astype(BF16)

    vt = z[:, wq + wk:].T
    pad = _ones_row_pad(tm)
    for g in range(N_KV_B):
        vt_ref[g] = jnp.concatenate([vt[g * HEAD_DIM:(g + 1) * HEAD_DIM, :], pad], axis=0).astype(BF16)


def _gqa_prep(zb, cos2, sin2, gq, gk, seg, L):
    T = zb.shape[0]
    per_b = L // TK
    wq = N_HEADS_B * HEAD_DIM
    wk = N_KV_B * HEAD_DIM
    return pl.pallas_call(
        _gqa_prep_kernel,
        grid=(T // TK,),
        in_specs=[
            pl.BlockSpec((TK, wq + 2 * wk), lambda i: (i, 0)),
            pl.BlockSpec((TK, LANES), lambda i: (i % per_b, 0)),
            pl.BlockSpec((TK, LANES), lambda i: (i % per_b, 0)),
            pl.BlockSpec((1, wq), lambda i: (0, 0)),
            pl.BlockSpec((1, wk), lambda i: (0, 0)),
            pl.BlockSpec((wq, wq), lambda i: (0, 0)),
        ],
        out_specs=[
            pl.BlockSpec((N_HEADS_B, HEAD_DIM, TK), lambda i: (0, 0, i)),
            pl.BlockSpec((N_KV_B, TK, HEAD_DIM), lambda i: (0, i, 0)),
            pl.BlockSpec((N_KV_B, None, HEAD_DIM + VPAD, TK), lambda i: (0, i, 0, 0)),
        ],
        out_shape=[
            jax.ShapeDtypeStruct((N_HEADS_B, HEAD_DIM, T), BF16),
            jax.ShapeDtypeStruct((N_KV_B, T, HEAD_DIM), BF16),
            jax.ShapeDtypeStruct((N_KV_B, T // TK, HEAD_DIM + VPAD, TK), BF16),
        ],
        compiler_params=_cparams(("parallel",)),
        name="gqa_prep",
    )(zb, cos2, sin2, gq, gk, seg)


def _ones_row_pad(n):
    row = lax.broadcasted_iota(jnp.int32, (VPAD, n), 0)
    return jnp.where(row == 0, 1.0, 0.0).astype(F32)


def _gqa_attn_kernel(qt_ref, k_ref, vt_ref, o_ref, s_ref, mc_ref, m_ref, acc_ref):
    tq = qt_ref.shape[2]
    nkv = vt_ref.shape[0]

    def scores(t, first):
        kt = k_ref[pl.ds(pl.multiple_of(t * TK, TK), TK), :]
        qt = jnp.concatenate([qt_ref[c] for c in range(GROUP_B)], axis=1)
        return _dot(kt, qt)

    _flash_loop(nkv, scores, lambda t: t, lambda t: 0.0, vt_ref, s_ref, mc_ref, m_ref, acc_ref)
    acc = acc_ref[...]
    o = acc[:HEAD_DIM] / acc[HEAD_DIM:HEAD_DIM + 1]
    for pair in range(GROUP_B // 2):
        two = jnp.concatenate([o[:, (2 * pair) * tq:(2 * pair + 1) * tq],
                               o[:, (2 * pair + 1) * tq:(2 * pair + 2) * tq]], axis=0)
        o_ref[:, pair * LANES:(pair + 1) * LANES] = two.T.astype(o_ref.dtype)


def _gqa_attn(q, k, vt, B, L):
    T = B * L
    tq = TQ_GQA
    nq = L // tq
    nkv = L // TK
    nqs = GROUP_B * tq
    dv = HEAD_DIM + VPAD
    return pl.pallas_call(
        _gqa_attn_kernel,
        grid=(B, N_KV_B, nq),
        in_specs=[
            pl.BlockSpec((GROUP_B, HEAD_DIM, tq), lambda b, g, i: (g, 0, b * nq + i)),
            pl.BlockSpec((None, L, HEAD_DIM), lambda b, g, i: (g, b, 0)),
            pl.BlockSpec((None, nkv, dv, TK), lambda b, g, i: (g, b, 0, 0)),
        ],
        out_specs=pl.BlockSpec((tq, GROUP_B * HEAD_DIM), lambda b, g, i: (b * nq + i, g)),
        out_shape=jax.ShapeDtypeStruct((T, N_HEADS_B * HEAD_DIM), BF16),
        scratch_shapes=[pltpu.VMEM((2, TK, nqs), F32), pltpu.VMEM((2, 1, nqs), F32),
                        pltpu.VMEM((1, nqs), F32), pltpu.VMEM((dv, nqs), F32)],
        compiler_params=_cparams(("parallel", "parallel", "arbitrary")),
        name="gqa_attn",
    )(q, k, vt)


def _na_kernel(q_ref, k_ref, v_ref, g_ref, kx_ref, qx_ref, o_ref, kaug_ref, vt_ref, s_ref, mc_ref, *, rows):
    i = pl.program_id(2)
    nk = NA_WIN * GRID_W
    r0 = i * NA_R
    ws = jnp.clip(r0 - NA_ROWS // 2, 0, rows - NA_WIN)
    koff = pl.multiple_of(ws * GRID_W, GRID_W)
    goff = pl.multiple_of((NA_R - (r0 - ws)) * GRID_W, GRID_W)
    kaug_ref[:, :LANES] = k_ref[pl.ds(koff, nk), :]
    kaug_ref[:, LANES:] = kx_ref[...]
    vt_ref[...] = jnp.concatenate([v_ref[pl.ds(koff, nk), :].astype(F32).T, _ones_row_pad(nk)],
                                  axis=0).astype(BF16)
    q = q_ref[...]
    lane = lax.broadcasted_iota(jnp.int32, q.shape, 1)
    for hh in range(2):
        in_head = (lane >= hh * HEAD_DIM) & (lane < (hh + 1) * HEAD_DIM)
        qaug = jnp.concatenate([jnp.where(in_head, q, jnp.zeros_like(q)), qx_ref[...]], axis=1)
        s = _dot_nt(kaug_ref[...], qaug) + g_ref[hh, pl.ds(goff, nk), :]
        s_ref[hh] = s
        mc_ref[hh] = jnp.max(s, axis=0, keepdims=True)
    outs = []
    for hh in range(2):
        p = jnp.exp2(s_ref[hh] - mc_ref[hh]).astype(BF16)
        acc = _dot(vt_ref[...], p)
        outs.append(acc[hh * HEAD_DIM:(hh + 1) * HEAD_DIM] / acc[2 * HEAD_DIM:2 * HEAD_DIM + 1])
    o_ref[...] = jnp.concatenate(outs, axis=0).T.astype(o_ref.dtype)


def _na_row_tables():
    e = np.arange(NA_R)
    jrow = np.arange(NA_WIN)
    half = NA_ROWS // 2
    lo = np.stack([np.maximum(e - half, 0), e, half + np.minimum(e, half)])
    valid = (jrow[None, :, None] >= lo[:, None, :]) & (jrow[None, :, None] < lo[:, None, :] + NA_ROWS)
    kx = np.zeros((3, NA_WIN, GRID_W, LANES), np.float32)
    kx[..., :NA_R] = np.where(valid, 0.0, -NA_BIG)[:, :, None, :]
    qx = np.zeros((NA_R, GRID_W, LANES), np.float32)
    qx[e, :, e] = 1.0
    return (jnp.asarray(kx.reshape(3, NA_WIN * GRID_W, LANES), BF16),
            jnp.asarray(qx.reshape(NA_R * GRID_W, LANES), BF16))


def _na_attn(za, g, B, L):
    T = B * L
    rows = L // GRID_W
    assert rows >= NA_WIN and rows % NA_R == 0
    nr = rows // NA_R
    nq = NA_R * GRID_W
    nk = NA_WIN * GRID_W
    npair = N_HEADS_A // 2
    kx, qx = _na_row_tables()
    return pl.pallas_call(
        functools.partial(_na_kernel, rows=rows),
        grid=(B, npair, nr),
        in_specs=[
            pl.BlockSpec((nq, LANES), lambda b, hp, i: (b * nr + i, hp)),
            pl.BlockSpec((L, LANES), lambda b, hp, i: (b, npair + hp)),
            pl.BlockSpec((L, LANES), lambda b, hp, i: (b, 2 * npair + hp)),
            pl.BlockSpec((2, g.shape[1], nq), lambda b, hp, i: (hp, 0, 0)),
            pl.BlockSpec((None, nk, LANES),
                         lambda b, hp, i: (jnp.where(i == 0, 0, jnp.where(i == nr - 1, 2, 1)), 0, 0)),
            pl.BlockSpec((nq, LANES), lambda b, hp, i: (0, 0)),
        ],
        out_specs=pl.BlockSpec((nq, LANES), lambda b, hp, i: (b * nr + i, hp)),
        out_shape=jax.ShapeDtypeStruct((T, N_HEADS_A * HEAD_DIM), BF16),
        scratch_shapes=[pltpu.VMEM((nk, 2 * LANES), BF16), pltpu.VMEM((2 * HEAD_DIM + VPAD, nk), BF16),
                        pltpu.VMEM((2, nk, nq), F32), pltpu.VMEM((2, 1, nq), F32)],
        compiler_params=_cparams(("parallel", "parallel", "arbitrary")),
        name="na_attn",
    )(za, za, za, g, kx, qx)


def _na_bias_table(rpb):
    col = np.arange(GRID_W)
    cs = np.clip(col - NA_COLS // 2, 0, GRID_W - NA_COLS)
    kc = np.arange(GRID_W)
    valid = (kc[:, None] >= cs[None, :]) & (kc[:, None] < cs[None, :] + NA_COLS)
    dcol = kc[:, None] - col[None, :] + (NA_COLS - 1)
    sel = ((dcol[None] == np.arange(2 * NA_COLS - 1)[:, None, None]) & valid[None]).astype(np.float32)
    band = jnp.einsum('hrd,dkc->hrkc', rpb.astype(F32), jnp.asarray(sel),
                      precision=lax.Precision.HIGHEST) * LOG2E
    band = jnp.where(valid[None, None], band, NEG_BIG)
    band = jnp.pad(band, ((0, 0), (NA_R, NA_R), (0, 0), (0, 0)))
    njj = NA_WIN + NA_R
    t = jnp.stack([band[:, NA_ROWS - 1 - qr:NA_ROWS - 1 - qr + njj] for qr in range(NA_R)], axis=1)
    t = t.transpose(0, 2, 3, 1, 4)
    return t.reshape(N_HEADS_A, njj * GRID_W, NA_R * GRID_W).astype(F32)


def _conv_kernel(z_ref, zp_ref, zn_ref, w_ref, b_ref, lg_ref, lb_ref, o_ref, u_scr):
    i = pl.program_id(1)
    nt = pl.num_programs(1)
    T = z_ref.shape[0]

    def glu(z):
        return z[:, :C_WIDTH] * jax.nn.sigmoid(z[:, C_WIDTH:])

    u_scr[CONV_HALO:CONV_HALO + T, :] = glu(z_ref[...])
    u_scr[:CONV_HALO, :] = jnp.where(i == 0, 0.0, glu(zp_ref[...]))
    u_scr[CONV_HALO + T:, :] = jnp.where(i == nt - 1, 0.0, glu(zn_ref[...]))

    off = CONV_HALO - CONV_K // 2
    half_w = C_WIDTH // 2
    for c in range(T // CONV_CHUNK):
        halves = []
        for lh in range(2):
            lanes = slice(lh * half_w, (lh + 1) * half_w)
            acc = None
            for r in range(SUBLANES):
                part = None
                for k in range(CONV_K):
                    if (off + k) % SUBLANES != r:
                        continue
                    start = c * CONV_CHUNK + (off + k) // SUBLANES * SUBLANES
                    term = w_ref[k:k + 1, lanes] * u_scr[start:start + CONV_CHUNK + SUBLANES, lanes]
                    part = term if part is None else part + term
                shifted = part[r:r + CONV_CHUNK]
                acc = shifted if acc is None else acc + shifted
            halves.append(acc)
        y = jnp.concatenate(halves, axis=1) + b_ref[...]
        mu = jnp.mean(y, axis=-1, keepdims=True)
        yc = y - mu
        var = jnp.mean(yc * yc, axis=-1, keepdims=True)
        y = yc * lax.rsqrt(var + LN_EPS) * lg_ref[...] + lb_ref[...]
        o_ref[c * CONV_CHUNK:(c + 1) * CONV_CHUNK, :] = _silu(y).astype(o_ref.dtype)


def _conv_branch(zc, w, b, lg, lb, B, L):
    T = B * L
    nt = L // CONV_T
    hb = CONV_T // CONV_HALO
    last = T // CONV_HALO - 1
    row = pl.BlockSpec((1, C_WIDTH), lambda bb, i: (0, 0))
    return pl.pallas_call(
        _conv_kernel,
        grid=(B, nt),
        in_specs=[
            pl.BlockSpec((CONV_T, 2 * C_WIDTH), lambda bb, i: (bb * nt + i, 0)),
            pl.BlockSpec((CONV_HALO, 2 * C_WIDTH),
                         lambda bb, i: (jnp.maximum((bb * nt + i) * hb - 1, 0), 0)),
            pl.BlockSpec((CONV_HALO, 2 * C_WIDTH),
                         lambda bb, i: (jnp.minimum((bb * nt + i + 1) * hb, last), 0)),
            pl.BlockSpec((32, C_WIDTH), lambda bb, i: (0, 0)),
            row, row, row,
        ],
        out_specs=pl.BlockSpec((CONV_T, C_WIDTH), lambda bb, i: (bb * nt + i, 0)),
        out_shape=jax.ShapeDtypeStruct((T, C_WIDTH), BF16),
        scratch_shapes=[pltpu.VMEM((CONV_T + 2 * CONV_HALO, C_WIDTH), F32)],
        compiler_params=_cparams(("parallel", "arbitrary")),
        name="conv_branch",
    )(zc, zc, zc, w, b, lg, lb)


def _vt_prep_kernel(v_ref, vt_ref):
    v = v_ref[...].astype(F32)
    pad = _ones_row_pad(v.shape[0])
    for h in range(N_HEADS_D):
        vt = v[:, h * D_V_DIFF:(h + 1) * D_V_DIFF].T
        vt_ref[h] = jnp.concatenate([vt, pad], axis=0).astype(BF16)


def _vt_prep(zq):
    T = zq.shape[0]
    w = N_HEADS_D * D_V_DIFF
    dv = D_V_DIFF + VPAD
    return pl.pallas_call(
        _vt_prep_kernel,
        grid=(T // TK,),
        in_specs=[pl.BlockSpec((TK, w), lambda i: (i, 2))],
        out_specs=pl.BlockSpec((N_HEADS_D, None, dv, TK), lambda i: (0, i, 0, 0)),
        out_shape=jax.ShapeDtypeStruct((N_HEADS_D, T // TK, dv, TK), BF16),
        compiler_params=_cparams(("parallel",)),
        name="vt_prep",
    )(zq)


def _diff_attn_kernel(slope_ref, q_ref, k_ref, vt_ref, lq1_ref, lk1_ref, lq2_ref, lk2_ref, sg_ref,
                      o_ref, qs_ref, ek_ref, dbias_ref, s_ref, mc_ref, m_ref, acc_ref, *, lam_init):
    h = pl.program_id(1)
    i = pl.program_id(2)
    tq = q_ref.shape[0]
    nkv = vt_ref.shape[0]
    assert tq == TK
    slope = slope_ref[h] * LOG2E

    @pl.when(i == 0)
    def _():
        lane = lax.broadcasted_iota(jnp.int32, (tq, LANES), 1)
        off = lax.broadcasted_iota(jnp.int32, (tq, LANES), 0).astype(F32) * slope
        hi = off.astype(BF16).astype(F32)
        mid = (off - hi).astype(BF16).astype(F32)
        lo = (off - hi - mid).astype(BF16).astype(F32)
        ek_ref[...] = jnp.where(lane == 0, hi, jnp.where(lane == 1, mid, jnp.where(lane == 2, lo,
                                jnp.where(lane < 6, 1.0, 0.0)))).astype(BF16)
        qx_before = jnp.where(lane < 3, 1.0, jnp.where(lane == 3, -hi, jnp.where(lane == 4, -mid,
                              jnp.where(lane == 5, -lo, 0.0))))
        for v, qx in enumerate((jnp.zeros_like(qx_before), qx_before, -qx_before)):
            qxt = qx.T.astype(BF16)
            qs_ref[v, LANES:, :tq] = qxt
            qs_ref[v, LANES:, tq:] = qxt
        rel = (lax.broadcasted_iota(jnp.int32, (TK, tq), 1) - lax.broadcasted_iota(jnp.int32, (TK, tq), 0))
        dbias_ref[...] = -slope * jnp.abs(rel).astype(F32)

    q = q_ref[...]
    zero = jnp.zeros_like(q)
    qlane = lax.broadcasted_iota(jnp.int32, q.shape, 1)
    q0t = jnp.where(qlane < HEAD_DIM, q, zero).astype(F32).T.astype(BF16)
    q1t = jnp.where(qlane >= HEAD_DIM, q, zero).astype(F32).T.astype(BF16)
    for v in range(3):
        qs_ref[v, :LANES, :tq] = q0t
        qs_ref[v, :LANES, tq:] = q1t

    def tile(t):
        return jnp.where(t == 0, i, t - 1 + (t - 1 >= i).astype(jnp.int32))

    def offset(t):
        return -slope * (jnp.abs(i - tile(t)) * TK).astype(F32)

    def scores(t, first):
        j = tile(t)
        kt = k_ref[pl.ds(pl.multiple_of(j * TK, TK), TK), :]
        kaug = jnp.concatenate([kt, ek_ref[...]], axis=1)
        if first:
            d = dbias_ref[...]
            return _dot(kaug, qs_ref[0]) + jnp.concatenate([d, d], axis=1)
        return _dot(kaug, qs_ref[jnp.where(j < i, 1, 2)])

    _flash_loop(nkv, scores, tile, offset, vt_ref, s_ref, mc_ref, m_ref, acc_ref)

    lam = (jnp.exp(jnp.sum(lq1_ref[...] * lk1_ref[...], axis=-1, keepdims=True))
           - jnp.exp(jnp.sum(lq2_ref[...] * lk2_ref[...], axis=-1, keepdims=True)) + lam_init)
    acc = acc_ref[...]
    o = acc[:D_V_DIFF] / acc[D_V_DIFF:D_V_DIFF + 1]
    od = (o[:, :tq] - lam * o[:, tq:]).T
    ms = jnp.mean(od * od, axis=-1, keepdims=True)
    od = od * lax.rsqrt(ms + EPS) * sg_ref[...] * (1.0 - lam_init)
    o_ref[...] = od.astype(o_ref.dtype)


def _diff_attn(zq, vt, lam_params, subln_g, B, L, lam_init):
    T = B * L
    tq = TQ_DIFF
    nq = L // tq
    nkv = L // TK
    nqs = 2 * tq
    dv = D_V_DIFF + VPAD
    slopes = jnp.asarray(ALIBI_SLOPES, F32)
    vec = pl.BlockSpec((1, HEAD_DIM), lambda b, h, i: (0, 0))
    return pl.pallas_call(
        functools.partial(_diff_attn_kernel, lam_init=lam_init),
        grid=(B, N_HEADS_D, nq),
        in_specs=[
            pl.BlockSpec(memory_space=pltpu.SMEM),
            pl.BlockSpec((tq, LANES), lambda b, h, i: (b * nq + i, h)),
            pl.BlockSpec((L, LANES), lambda b, h, i: (b, N_HEADS_D + h)),
            pl.BlockSpec((None, nkv, dv, TK), lambda b, h, i: (h, b, 0, 0)),
            vec, vec, vec, vec,
            pl.BlockSpec((1, D_V_DIFF), lambda b, h, i: (0, 0)),
        ],
        out_specs=pl.BlockSpec((tq, LANES), lambda b, h, i: (b * nq + i, h)),
        out_shape=jax.ShapeDtypeStruct((T, N_HEADS_D * D_V_DIFF), BF16),
        scratch_shapes=[pltpu.VMEM((3, 2 * LANES, nqs), BF16), pltpu.VMEM((TK, LANES), BF16),
                        pltpu.VMEM((TK, tq), F32),
                        pltpu.VMEM((2, TK, nqs), F32), pltpu.VMEM((2, 1, nqs), F32),
                        pltpu.VMEM((1, nqs), F32), pltpu.VMEM((dv, nqs), F32)],
        compiler_params=_cparams(("parallel", "parallel", "arbitrary")),
        name="diff_attn",
    )(slopes, zq, zq, vt, *lam_params, subln_g)


def _rope_tables(L):
    t = np.arange(L)
    row = (t // GRID_W).astype(np.float64)
    col = (t % GRID_W).astype(np.float64)
    half = HEAD_DIM // 2
    inv = ROPE_THETA ** (-np.arange(0, half, 2, dtype=np.float64) / half)
    ang = np.concatenate([row[:, None] * inv, col[:, None] * inv], axis=-1)
    cos = np.cos(ang).astype(np.float32)
    sin = np.sin(ang).astype(np.float32)
    cos_h = np.concatenate([cos, cos], axis=-1)
    sin_h = np.concatenate([-sin, sin], axis=-1)
    return (jnp.asarray(np.tile(cos_h, (1, LANES // HEAD_DIM))),
            jnp.asarray(np.tile(sin_h, (1, LANES // HEAD_DIM))))


_DEINT = np.concatenate([np.arange(0, HEAD_DIM, 2), np.arange(1, HEAD_DIM, 2)])


def _prep_weights(p):
    c = SCALE * LOG2E
    w = {}
    wa = N_HEADS_A * HEAD_DIM
    wq = N_HEADS_B * HEAD_DIM
    wk = N_KV_B * HEAD_DIM
    w_ab = p['w_in_ab']
    qcols = (np.arange(N_HEADS_B)[:, None] * HEAD_DIM + _DEINT[None, :]).reshape(-1)
    kcols = (np.arange(N_KV_B)[:, None] * HEAD_DIM + _DEINT[None, :]).reshape(-1)
    w['w_in_ab'] = jnp.concatenate([
        w_ab[:, :, :wa] * c,
        w_ab[:, :, wa:3 * wa],
        w_ab[:, :, 3 * wa:3 * wa + wq][:, :, qcols],
        w_ab[:, :, 3 * wa + wq:3 * wa + wq + wk][:, :, kcols],
        w_ab[:, :, 3 * wa + wq + wk:],
    ], axis=-1).astype(BF16)
    w['gq'] = jnp.tile(p['qnorm_b'][:, _DEINT] * c, (1, N_HEADS_B))[:, None, :]
    w['gk'] = jnp.tile(p['knorm_b'][:, _DEINT], (1, N_KV_B))[:, None, :]
    w_cd = p['w_in_cd']
    wd = N_HEADS_D * 2 * HEAD_DIM
    w['w_in_cd'] = jnp.concatenate([
        w_cd[:, :, :2 * C_WIDTH],
        w_cd[:, :, 2 * C_WIDTH:2 * C_WIDTH + wd] * c,
        w_cd[:, :, 2 * C_WIDTH + wd:],
    ], axis=-1).astype(BF16)
    for name in ('w_out_ab', 'w_out_cd', 'w1', 'w3', 'w2'):
        w[name] = p[name].astype(BF16)
    seg = np.kron(np.eye(N_HEADS_B, dtype=np.float32), np.ones((HEAD_DIM, HEAD_DIM), np.float32))
    w['seg'] = jnp.asarray(seg, BF16)
    w['na_bias'] = [_na_bias_table(p['rpb_a'][j]) for j in range(p['rpb_a'].shape[0])]
    w['conv_w'] = jnp.pad(p['conv_w_c'], ((0, 0), (0, 32 - CONV_K), (0, 0)))
    return w


def _trunk(x3, mod, p, w, rope):
    B, L, _ = x3.shape
    x = x3.reshape(B * L, D_MODEL)
    wa = N_HEADS_A * HEAD_DIM
    for li in range(DEPTH):
        sh1, sc1, g1, sh2, sc2, g2 = [mod[li, :, k] for k in range(6)]
        j = li // 2
        if li % 2 == 0:
            za, zb = _inproj(x, p['norm_mix_g'][li], sc1, sh1, w['w_in_ab'][j], L,
                             (3 * wa, (N_HEADS_B + 2 * N_KV_B) * HEAD_DIM), (BF16, F32))
            oa = _na_attn(za, w['na_bias'][j], B, L)
            q, k, vt = _gqa_prep(zb, rope[0], rope[1], w['gq'][j], w['gk'][j], w['seg'], L)
            mix, w_out = (oa, _gqa_attn(q, k, vt, B, L)), w['w_out_ab'][j]
        else:
            zc, zq = _inproj(x, p['norm_mix_g'][li], sc1, sh1, w['w_in_cd'][j], L,
                             (2 * C_WIDTH, 3 * N_HEADS_D * D_V_DIFF), (F32, BF16))
            u = _conv_branch(zc, w['conv_w'][j], p['conv_b_c'][j][None], p['conv_ln_g'][j][None],
                             p['conv_ln_b'][j][None], B, L)
            lam_init = 0.8 - 0.6 * math.exp(-0.3 * li)
            lam_params = [p[n][j][None] for n in ('lam_q1', 'lam_k1', 'lam_q2', 'lam_k2')]
            od = _diff_attn(zq, _vt_prep(zq), lam_params, p['subln_g'][j][None], B, L, lam_init)
            mix, w_out = (u, od), w['w_out_cd'][j]
        x = _out_ffn(x, mix[0], mix[1], g1, w_out, p['norm_ffn_g'][li], sc2, sh2, g2,
                     w['w1'][li], w['w3'][li], w['w2'][li], p['final_g'], L, final=(li == DEPTH - 1))
    return x.reshape(B, L, D_MODEL)


def kernel(x_prompt, x_sample, c_prompt, c_sample, w_mod, b_mod, norm_mix_g, norm_ffn_g, w_in_ab, rpb_a,
           qnorm_b, knorm_b, w_out_ab, w_in_cd, conv_w_c, conv_b_c, conv_ln_g, conv_ln_b, lam_q1, lam_k1,
           lam_q2, lam_k2, subln_g, w_out_cd, w1, w3, w2, final_g):
    p = dict(norm_mix_g=norm_mix_g, norm_ffn_g=norm_ffn_g, w_in_ab=w_in_ab, rpb_a=rpb_a, qnorm_b=qnorm_b,
             knorm_b=knorm_b, w_out_ab=w_out_ab, w_in_cd=w_in_cd, conv_w_c=conv_w_c, conv_b_c=conv_b_c,
             conv_ln_g=conv_ln_g, conv_ln_b=conv_ln_b, lam_q1=lam_q1, lam_k1=lam_k1, lam_q2=lam_q2,
             lam_k2=lam_k2, subln_g=subln_g, w_out_cd=w_out_cd, w1=w1, w3=w3, w2=w2, final_g=final_g)
    w = _prep_weights(p)
    bp, bs = c_prompt.shape[0], c_sample.shape[0]
    rows = -(-(bp + bs) // 8) * 8
    c_all = jnp.concatenate([c_prompt, c_sample, jnp.zeros((rows - bp - bs, D_MODEL), F32)], axis=0)
    mod = _modulation(c_all, w_mod, b_mod).reshape(DEPTH, rows, 6, 1, D_MODEL)
    y_prompt = _trunk(x_prompt, mod[:, :bp], p, w, _rope_tables(x_prompt.shape[1]))
    y_sample = _trunk(x_sample, mod[:, bp:bp + bs], p, w, _rope_tables(x_sample.shape[1]))
    return (y_prompt, y_sample)
```

```python
import functools
import math

import numpy as np
import jax
import jax.numpy as jnp
from jax import lax
from jax.experimental import pallas as pl
from jax.experimental.pallas import tpu as pltpu

F32 = jnp.float32
BF16 = jnp.bfloat16

D_MODEL = 1024
DEPTH = 4
GRID_W = 64
HEAD_DIM = 64
N_HEADS_A = 8
NA_ROWS = 8
NA_COLS = 16
N_HEADS_B = 8
N_KV_B = 2
GROUP_B = N_HEADS_B // N_KV_B
ROPE_THETA = 10000.0
C_WIDTH = 512
CONV_K = 31
N_HEADS_D = 4
D_V_DIFF = 2 * HEAD_DIM
D_FF = 2816
EPS = 1e-6
LN_EPS = 1e-5
ALIBI_SLOPES = tuple(2.0 ** (-8.0 * (h + 1) / N_HEADS_D) for h in range(N_HEADS_D))
LOG2E = math.log2(math.e)
SCALE = HEAD_DIM ** -0.5
NEG_BIG = -1e30

LANES = 128
SUBLANES = 8
VMEM_LIMIT = 56 * 1024 * 1024

TM = 512
MXU_TILE = 256
FF_CHUNKS = ((0, 4 * MXU_TILE), (4 * MXU_TILE, 8 * MXU_TILE), (8 * MXU_TILE, D_FF))
TK = 512
TQ_GQA = 256
VPAD = 16
TQ_DIFF = TK
NA_R = 8
NA_WIN = 2 * NA_R
NA_BIG = 2.0 ** 100
CONV_T = 256
CONV_HALO = 16
CONV_CHUNK = 64


def _cparams(sem):
    return pltpu.CompilerParams(dimension_semantics=sem, vmem_limit_bytes=VMEM_LIMIT)


def _dot(a, b):
    return jnp.dot(a, b, preferred_element_type=F32)


def _dot_nt(a, b):
    return lax.dot_general(a, b, (((1,), (1,)), ((), ())), preferred_element_type=F32)


def _silu(x):
    return x * jax.nn.sigmoid(x)


def _rms_mod(x, g, sc, sh):
    ms = jnp.mean(x * x, axis=-1, keepdims=True)
    h = x * lax.rsqrt(ms + EPS) * g
    return h * (1.0 + sc) + sh


def _mod_kernel(c_ref, w_ref, b_ref, o_ref):
    cs = _silu(c_ref[...])
    o_ref[...] = _dot(cs.astype(BF16), w_ref[...].astype(BF16)) + b_ref[...]


def _modulation(c_all, w_mod, b_mod):
    R = c_all.shape[0]
    tn = 1536
    return pl.pallas_call(
        _mod_kernel,
        grid=(DEPTH, 6 * D_MODEL // tn),
        in_specs=[
            pl.BlockSpec((R, D_MODEL), lambda l, j: (0, 0)),
            pl.BlockSpec((None, D_MODEL, tn), lambda l, j: (l, 0, j)),
            pl.BlockSpec((None, 1, tn), lambda l, j: (l, 0, j)),
        ],
        out_specs=pl.BlockSpec((None, R, tn), lambda l, j: (l, 0, j)),
        out_shape=jax.ShapeDtypeStruct((DEPTH, R, 6 * D_MODEL), F32),
        compiler_params=_cparams(("arbitrary", "arbitrary")),
        name="modulation",
    )(c_all, w_mod, b_mod.reshape(DEPTH, 1, 6 * D_MODEL))


def _inproj_kernel(x_ref, g_ref, sc_ref, sh_ref, w_ref, *o_refs, splits, vt_heads):
    h = _rms_mod(x_ref[...], g_ref[...], sc_ref[...], sh_ref[...])
    z = _dot(h.astype(BF16), w_ref[...])
    off = 0
    for o_ref, n in zip(o_refs, splits):
        o_ref[...] = z[:, off:off + n].astype(o_ref.dtype)
        off += n
    if vt_heads:
        vt_ref = o_refs[len(splits)]
        pad = _ones_row_pad(z.shape[0])
        for hd in range(vt_heads):
            vt = z[:, off + hd * D_V_DIFF:off + (hd + 1) * D_V_DIFF].T
            vt_ref[hd] = jnp.concatenate([vt, pad], axis=0).astype(BF16)


def _inproj(x, g, sc, sh, w, L, splits, dtypes, vt_heads=0):
    T = x.shape[0]
    N = w.shape[1]
    assert sum(splits) + vt_heads * D_V_DIFF == N and TM == TK
    per_b = L // TM
    vec = pl.BlockSpec((None, 1, D_MODEL), lambda i: (i // per_b, 0, 0))
    out_specs = [pl.BlockSpec((TM, n), lambda i: (i, 0)) for n in splits]
    out_shape = [jax.ShapeDtypeStruct((T, n), dt) for n, dt in zip(splits, dtypes)]
    if vt_heads:
        dv = D_V_DIFF + VPAD
        out_specs.append(pl.BlockSpec((vt_heads, None, dv, TK), lambda i: (0, i, 0, 0)))
        out_shape.append(jax.ShapeDtypeStruct((vt_heads, T // TK, dv, TK), BF16))
    return pl.pallas_call(
        functools.partial(_inproj_kernel, splits=splits, vt_heads=vt_heads),
        grid=(T // TM,),
        in_specs=[
            pl.BlockSpec((TM, D_MODEL), lambda i: (i, 0)),
            pl.BlockSpec((1, D_MODEL), lambda i: (0, 0)),
            vec, vec,
            pl.BlockSpec((D_MODEL, N), lambda i: (0, 0)),
        ],
        out_specs=out_specs,
        out_shape=out_shape,
        compiler_params=_cparams(("parallel",)),
        name="inproj",
    )(x, g.reshape(1, D_MODEL), sc, sh, w)


def _out_ffn_kernel(x_ref, a_ref, b_ref, gate1_ref, wo_ref, g_ref, sc_ref, sh_ref, gate2_ref,
                    w1_ref, w3_ref, w2_ref, fg_ref, o_ref, *, final):
    half = a_ref.shape[1]
    y = _dot(a_ref[...], wo_ref[:half, :]) + _dot(b_ref[...], wo_ref[half:, :])
    x1 = x_ref[...] + (1.0 + gate1_ref[...]) * y
    h = _rms_mod(x1, g_ref[...], sc_ref[...], sh_ref[...]).astype(BF16)
    acc = None
    for lo, hi in FF_CHUNKS:
        t = _silu(_dot(h, w1_ref[:, lo:hi])) * _dot(h, w3_ref[:, lo:hi])
        part = _dot(t.astype(BF16), w2_ref[lo:hi, :])
        acc = part if acc is None else acc + part
    out = x1 + (1.0 + gate2_ref[...]) * acc
    if final:
        ms = jnp.mean(out * out, axis=-1, keepdims=True)
        out = out * lax.rsqrt(ms + EPS) * fg_ref[...]
    o_ref[...] = out


def _out_ffn(x, a, b, gate1, wo, g, sc, sh, gate2, w1, w3, w2, final_g, L, final):
    T = x.shape[0]
    per_b = L // TM
    half = a.shape[1]
    vec = pl.BlockSpec((None, 1, D_MODEL), lambda i: (i // per_b, 0, 0))
    row = pl.BlockSpec((1, D_MODEL), lambda i: (0, 0))

    def resident(shape):
        return pl.BlockSpec(shape, lambda i: (0, 0), pipeline_mode=pl.Buffered(1))

    return pl.pallas_call(
        functools.partial(_out_ffn_kernel, final=final),
        grid=(T // TM,),
        in_specs=[
            pl.BlockSpec((TM, D_MODEL), lambda i: (i, 0)),
            pl.BlockSpec((TM, half), lambda i: (i, 0)),
            pl.BlockSpec((TM, half), lambda i: (i, 0)),
            vec,
            resident((2 * half, D_MODEL)),
            row, vec, vec, vec,
            resident((D_MODEL, D_FF)), resident((D_MODEL, D_FF)), resident((D_FF, D_MODEL)),
            row,
        ],
        out_specs=pl.BlockSpec((TM, D_MODEL), lambda i: (i, 0)),
        out_shape=jax.ShapeDtypeStruct((T, D_MODEL), F32),
        compiler_params=_cparams(("parallel",)),
        name="out_ffn",
    )(x, a, b, gate1, wo, g.reshape(1, D_MODEL), sc, sh, gate2, w1, w3, w2, final_g.reshape(1, D_MODEL))


def _flash_loop(nkv, score_fn, tile_fn, offset_fn, vt_ref, s_ref, mc_ref, m_ref, acc_ref):
    unroll = max([u for u in (4, 8) if nkv % u == 0 and nkv // u >= 2], default=2)
    assert nkv % unroll == 0

    def produce(t, slot, first=False):
        s = score_fn(t, first)
        s_ref[slot] = s
        mc_ref[slot] = jnp.max(s, axis=0, keepdims=True) + offset_fn(t)

    def consume(t, slot):
        m_prev = m_ref[...]
        m_new = jnp.maximum(m_prev, mc_ref[slot])
        alpha = jnp.exp2(m_prev - m_new)
        p = jnp.exp2(s_ref[slot] - (m_new - offset_fn(t))).astype(BF16)
        acc_ref[...] = alpha * acc_ref[...] + _dot(vt_ref[tile_fn(t)], p)
        m_ref[...] = m_new

    m_ref[...] = jnp.full_like(m_ref, NEG_BIG)
    acc_ref[...] = jnp.zeros_like(acc_ref)
    produce(0, 0, first=True)

    def span(t0, n, last):
        for u in range(n):
            if not (last and u == n - 1):
                produce(t0 + u + 1, (u + 1) % 2)
            consume(t0 + u, u % 2)

    def body(tt, carry):
        span(unroll * tt, unroll, False)
        return carry

    trips = nkv // unroll - 1
    if trips == 1:
        trips = trips + jnp.minimum(pl.program_id(0), 0)
    lax.fori_loop(0, trips, body, 0)
    span(nkv - unroll, unroll, True)


def _seg_mean_sq(x, seg_ref):
    sq = x * x
    hi = sq.astype(BF16)
    lo = (sq - hi.astype(F32)).astype(BF16)
    n = x.shape[1]
    seg = seg_ref[:n, :n]
    return (_dot(hi, seg) + _dot(lo, seg)) * (1.0 / HEAD_DIM)


def _rope(x, cos, sin_signed, first_half):
    n = x.shape[1]
    half = HEAD_DIM // 2
    rot = jnp.where(first_half, pltpu.roll(x, n - half, 1), pltpu.roll(x, half, 1))
    return x * cos + rot * sin_signed


def _gqa_prep(z, cos2, sin2, gq_ref, gk_ref, seg_ref, qt_ref, k_ref, vt_ref):
    wq = N_HEADS_B * HEAD_DIM
    wk = N_KV_B * HEAD_DIM
    tm = z.shape[0]

    def first_half(width):
        lane = lax.broadcasted_iota(jnp.int32, (tm, width), 1)
        return (lane % HEAD_DIM) < (HEAD_DIM // 2)

    xq = z[:, :wq]
    xq = xq * lax.rsqrt(_seg_mean_sq(xq, seg_ref) + EPS) * gq_ref[...]
    cosq = jnp.concatenate([cos2] * (wq // LANES), axis=1)
    sinq = jnp.concatenate([sin2] * (wq // LANES), axis=1)
    q = _rope(xq, cosq, sinq, first_half(wq))
    for pair in range(N_HEADS_B // 2):
        qt = q[:, pair * LANES:(pair + 1) * LANES].T
        qt_ref[2 * pair] = qt[:HEAD_DIM].astype(BF16)
        qt_ref[2 * pair + 1] = qt[HEAD_DIM:].astype(BF16)

    xk = z[:, wq:wq + wk]
    xk = xk * lax.rsqrt(_seg_mean_sq(xk, seg_ref) + EPS) * gk_ref[...]
    k = _rope(xk, cos2, sin2, first_half(wk))
    for g in range(N_KV_B):
        k_ref[g] = k[:, g * HEAD_DIM:(g + 1) * HEAD_DIM].astype(BF16)

    vt = z[:, wq + wk:].T
    pad = _ones_row_pad(tm)
    for g in range(N_KV_B):
        vt_ref[g] = jnp.concatenate([vt[g * HEAD_DIM:(g + 1) * HEAD_DIM, :], pad], axis=0).astype(BF16)


def _inproj_ab_kernel(x_ref, g_ref, sc_ref, sh_ref, w_ref, cos_ref, sin_ref, gq_ref, gk_ref, seg_ref,
                      za_ref, qt_ref, k_ref, vt_ref):
    h = _rms_mod(x_ref[...], g_ref[...], sc_ref[...], sh_ref[...])
    z = _dot(h.astype(BF16), w_ref[...])
    wa3 = za_ref.shape[1]
    za_ref[...] = z[:, :wa3].astype(BF16)
    _gqa_prep(z[:, wa3:], cos_ref[...], sin_ref[...], gq_ref, gk_ref, seg_ref, qt_ref, k_ref, vt_ref)


def _inproj_ab(x, g, sc, sh, w, cos2, sin2, gq, gk, seg, L):
    T = x.shape[0]
    assert TM == TK
    per_b = L // TM
    wa3 = 3 * N_HEADS_A * HEAD_DIM
    wq = N_HEADS_B * HEAD_DIM
    wk = N_KV_B * HEAD_DIM
    vec = pl.BlockSpec((None, 1, D_MODEL), lambda i: (i // per_b, 0, 0))
    return pl.pallas_call(
        _inproj_ab_kernel,
        grid=(T // TM,),
        in_specs=[
            pl.BlockSpec((TM, D_MODEL), lambda i: (i, 0)),
            pl.BlockSpec((1, D_MODEL), lambda i: (0, 0)),
            vec, vec,
            pl.BlockSpec((D_MODEL, w.shape[1]), lambda i: (0, 0)),
            pl.BlockSpec((TM, LANES), lambda i: (i % per_b, 0)),
            pl.BlockSpec((TM, LANES), lambda i: (i % per_b, 0)),
            pl.BlockSpec((1, wq), lambda i: (0, 0)),
            pl.BlockSpec((1, wk), lambda i: (0, 0)),
            pl.BlockSpec((wq, wq), lambda i: (0, 0)),
        ],
        out_specs=[
            pl.BlockSpec((TM, wa3), lambda i: (i, 0)),
            pl.BlockSpec((N_HEADS_B, HEAD_DIM, TM), lambda i: (0, 0, i)),
            pl.BlockSpec((N_KV_B, TM, HEAD_DIM), lambda i: (0, i, 0)),
            pl.BlockSpec((N_KV_B, None, HEAD_DIM + VPAD, TK), lambda i: (0, i, 0, 0)),
        ],
        out_shape=[
            jax.ShapeDtypeStruct((T, wa3), BF16),
            jax.ShapeDtypeStruct((N_HEADS_B, HEAD_DIM, T), BF16),
            jax.ShapeDtypeStruct((N_KV_B, T, HEAD_DIM), BF16),
            jax.ShapeDtypeStruct((N_KV_B, T // TK, HEAD_DIM + VPAD, TK), BF16),
        ],
        compiler_params=_cparams(("parallel",)),
        name="inproj_ab",
    )(x, g.reshape(1, D_MODEL), sc, sh, w, cos2, sin2, gq, gk, seg)


def _ones_row_pad(n):
    row = lax.broadcasted_iota(jnp.int32, (VPAD, n), 0)
    return jnp.where(row == 0, 1.0, 0.0).astype(F32)


def _gqa_attn_kernel(qt_ref, k_ref, vt_ref, o_ref, s_ref, mc_ref, m_ref, acc_ref):
    tq = qt_ref.shape[2]
    nkv = vt_ref.shape[0]

    def scores(t, first):
        kt = k_ref[pl.ds(pl.multiple_of(t * TK, TK), TK), :]
        qt = jnp.concatenate([qt_ref[c] for c in range(GROUP_B)], axis=1)
        return _dot(kt, qt)

    _flash_loop(nkv, scores, lambda t: t, lambda t: 0.0, vt_ref, s_ref, mc_ref, m_ref, acc_ref)
    acc = acc_ref[...]
    o = acc[:HEAD_DIM] / acc[HEAD_DIM:HEAD_DIM + 1]
    for pair in range(GROUP_B // 2):
        two = jnp.concatenate([o[:, (2 * pair) * tq:(2 * pair + 1) * tq],
                               o[:, (2 * pair + 1) * tq:(2 * pair + 2) * tq]], axis=0)
        o_ref[:, pair * LANES:(pair + 1) * LANES] = two.T.astype(o_ref.dtype)


def _gqa_attn(q, k, vt, B, L):
    T = B * L
    tq = TQ_GQA
    nq = L // tq
    nkv = L // TK
    nqs = GROUP_B * tq
    dv = HEAD_DIM + VPAD
    return pl.pallas_call(
        _gqa_attn_kernel,
        grid=(B, N_KV_B, nq),
        in_specs=[
            pl.BlockSpec((GROUP_B, HEAD_DIM, tq), lambda b, g, i: (g, 0, b * nq + i)),
            pl.BlockSpec((None, L, HEAD_DIM), lambda b, g, i: (g, b, 0)),
            pl.BlockSpec((None, nkv, dv, TK), lambda b, g, i: (g, b, 0, 0)),
        ],
        out_specs=pl.BlockSpec((tq, GROUP_B * HEAD_DIM), lambda b, g, i: (b * nq + i, g)),
        out_shape=jax.ShapeDtypeStruct((T, N_HEADS_B * HEAD_DIM), BF16),
        scratch_shapes=[pltpu.VMEM((2, TK, nqs), F32), pltpu.VMEM((2, 1, nqs), F32),
                        pltpu.VMEM((1, nqs), F32), pltpu.VMEM((dv, nqs), F32)],
        compiler_params=_cparams(("parallel", "parallel", "arbitrary")),
        name="gqa_attn",
    )(q, k, vt)


def _na_kernel(q_ref, k_ref, v_ref, g_ref, kx_ref, qx_ref, o_ref, kaug_ref, vt_ref, s_ref, mc_ref, *, rows):
    i = pl.program_id(2)
    nk = NA_WIN * GRID_W
    r0 = i * NA_R
    ws = jnp.clip(r0 - NA_ROWS // 2, 0, rows - NA_WIN)
    koff = pl.multiple_of(ws * GRID_W, GRID_W)
    goff = pl.multiple_of((NA_R - (r0 - ws)) * GRID_W, GRID_W)
    kaug_ref[:, :LANES] = k_ref[pl.ds(koff, nk), :]
    kaug_ref[:, LANES:] = kx_ref[...]
    vt_ref[...] = jnp.concatenate([v_ref[pl.ds(koff, nk), :].astype(F32).T, _ones_row_pad(nk)],
                                  axis=0).astype(BF16)
    q = q_ref[...]
    lane = lax.broadcasted_iota(jnp.int32, q.shape, 1)
    for hh in range(2):
        in_head = (lane >= hh * HEAD_DIM) & (lane < (hh + 1) * HEAD_DIM)
        qaug = jnp.concatenate([jnp.where(in_head, q, jnp.zeros_like(q)), qx_ref[...]], axis=1)
        for c in range(q.shape[0] // MXU_TILE):
            cols = slice(c * MXU_TILE, (c + 1) * MXU_TILE)
            s = _dot_nt(kaug_ref[...], qaug[cols]) + g_ref[hh, pl.ds(goff, nk), cols]
            s_ref[hh, :, cols] = s
            mc_ref[hh, :, cols] = jnp.max(s, axis=0, keepdims=True)
    outs = []
    for hh in range(2):
        parts = []
        for c in range(q.shape[0] // MXU_TILE):
            cols = slice(c * MXU_TILE, (c + 1) * MXU_TILE)
            p = jnp.exp2(s_ref[hh, :, cols] - mc_ref[hh, :, cols]).astype(BF16)
            acc = _dot(vt_ref[...], p)
            parts.append(acc[hh * HEAD_DIM:(hh + 1) * HEAD_DIM] / acc[2 * HEAD_DIM:2 * HEAD_DIM + 1])
        outs.append(jnp.concatenate(parts, axis=1))
    o_ref[...] = jnp.concatenate(outs, axis=0).T.astype(o_ref.dtype)


def _na_row_tables():
    e = np.arange(NA_R)
    jrow = np.arange(NA_WIN)
    half = NA_ROWS // 2
    lo = np.stack([np.maximum(e - half, 0), e, half + np.minimum(e, half)])
    valid = (jrow[None, :, None] >= lo[:, None, :]) & (jrow[None, :, None] < lo[:, None, :] + NA_ROWS)
    kx = np.zeros((3, NA_WIN, GRID_W, LANES), np.float32)
    kx[..., :NA_R] = np.where(valid, 0.0, -NA_BIG)[:, :, None, :]
    qx = np.zeros((NA_R, GRID_W, LANES), np.float32)
    qx[e, :, e] = 1.0
    return (jnp.asarray(kx.reshape(3, NA_WIN * GRID_W, LANES), BF16),
            jnp.asarray(qx.reshape(NA_R * GRID_W, LANES), BF16))


def _na_attn(za, g, B, L):
    T = B * L
    rows = L // GRID_W
    assert rows >= NA_WIN and rows % NA_R == 0
    nr = rows // NA_R
    nq = NA_R * GRID_W
    nk = NA_WIN * GRID_W
    npair = N_HEADS_A // 2
    kx, qx = _na_row_tables()
    return pl.pallas_call(
        functools.partial(_na_kernel, rows=rows),
        grid=(B, npair, nr),
        in_specs=[
            pl.BlockSpec((nq, LANES), lambda b, hp, i: (b * nr + i, hp)),
            pl.BlockSpec((L, LANES), lambda b, hp, i: (b, npair + hp)),
            pl.BlockSpec((L, LANES), lambda b, hp, i: (b, 2 * npair + hp)),
            pl.BlockSpec((2, g.shape[1], nq), lambda b, hp, i: (hp, 0, 0)),
            pl.BlockSpec((None, nk, LANES),
                         lambda b, hp, i: (jnp.where(i == 0, 0, jnp.where(i == nr - 1, 2, 1)), 0, 0)),
            pl.BlockSpec((nq, LANES), lambda b, hp, i: (0, 0)),
        ],
        out_specs=pl.BlockSpec((nq, LANES), lambda b, hp, i: (b * nr + i, hp)),
        out_shape=jax.ShapeDtypeStruct((T, N_HEADS_A * HEAD_DIM), BF16),
        scratch_shapes=[pltpu.VMEM((nk, 2 * LANES), BF16), pltpu.VMEM((2 * HEAD_DIM + VPAD, nk), BF16),
                        pltpu.VMEM((2, nk, nq), F32), pltpu.VMEM((2, 1, nq), F32)],
        compiler_params=_cparams(("parallel", "parallel", "arbitrary")),
        name="na_attn",
    )(za, za, za, g, kx, qx)


def _na_bias_table(rpb):
    col = np.arange(GRID_W)
    cs = np.clip(col - NA_COLS // 2, 0, GRID_W - NA_COLS)
    kc = np.arange(GRID_W)
    valid = (kc[:, None] >= cs[None, :]) & (kc[:, None] < cs[None, :] + NA_COLS)
    dcol = kc[:, None] - col[None, :] + (NA_COLS - 1)
    sel = ((dcol[None] == np.arange(2 * NA_COLS - 1)[:, None, None]) & valid[None]).astype(np.float32)
    band = jnp.einsum('hrd,dkc->hrkc', rpb.astype(F32), jnp.asarray(sel),
                      precision=lax.Precision.HIGHEST) * LOG2E
    band = jnp.where(valid[None, None], band, NEG_BIG)
    band = jnp.pad(band, ((0, 0), (NA_R, NA_R), (0, 0), (0, 0)))
    njj = NA_WIN + NA_R
    t = jnp.stack([band[:, NA_ROWS - 1 - qr:NA_ROWS - 1 - qr + njj] for qr in range(NA_R)], axis=1)
    t = t.transpose(0, 2, 3, 1, 4)
    return t.reshape(N_HEADS_A, njj * GRID_W, NA_R * GRID_W).astype(F32)


def _conv_kernel(z_ref, zp_ref, zn_ref, w_ref, b_ref, lg_ref, lb_ref, o_ref, u_scr):
    i = pl.program_id(1)
    nt = pl.num_programs(1)
    T = z_ref.shape[0]

    def glu(z):
        return z[:, :C_WIDTH] * jax.nn.sigmoid(z[:, C_WIDTH:])

    u_scr[CONV_HALO:CONV_HALO + T, :] = glu(z_ref[...])
    u_scr[:CONV_HALO, :] = jnp.where(i == 0, 0.0, glu(zp_ref[...]))
    u_scr[CONV_HALO + T:, :] = jnp.where(i == nt - 1, 0.0, glu(zn_ref[...]))

    off = CONV_HALO - CONV_K // 2
    half_w = C_WIDTH // 2
    for c in range(T // CONV_CHUNK):
        halves = []
        for lh in range(2):
            lanes = slice(lh * half_w, (lh + 1) * half_w)
            acc = None
            for r in range(SUBLANES):
                part = None
                for k in range(CONV_K):
                    if (off + k) % SUBLANES != r:
                        continue
                    start = c * CONV_CHUNK + (off + k) // SUBLANES * SUBLANES
                    term = w_ref[k:k + 1, lanes] * u_scr[start:start + CONV_CHUNK + SUBLANES, lanes]
                    part = term if part is None else part + term
                shifted = part[r:r + CONV_CHUNK]
                acc = shifted if acc is None else acc + shifted
            halves.append(acc)
        y = jnp.concatenate(halves, axis=1) + b_ref[...]
        mu = jnp.mean(y, axis=-1, keepdims=True)
        yc = y - mu
        var = jnp.mean(yc * yc, axis=-1, keepdims=True)
        y = yc * lax.rsqrt(var + LN_EPS) * lg_ref[...] + lb_ref[...]
        o_ref[c * CONV_CHUNK:(c + 1) * CONV_CHUNK, :] = _silu(y).astype(o_ref.dtype)


def _conv_branch(zc, w, b, lg, lb, B, L):
    T = B * L
    nt = L // CONV_T
    hb = CONV_T // CONV_HALO
    last = T // CONV_HALO - 1
    row = pl.BlockSpec((1, C_WIDTH), lambda bb, i: (0, 0))
    return pl.pallas_call(
        _conv_kernel,
        grid=(B, nt),
        in_specs=[
            pl.BlockSpec((CONV_T, 2 * C_WIDTH), lambda bb, i: (bb * nt + i, 0)),
            pl.BlockSpec((CONV_HALO, 2 * C_WIDTH),
                         lambda bb, i: (jnp.maximum((bb * nt + i) * hb - 1, 0), 0)),
            pl.BlockSpec((CONV_HALO, 2 * C_WIDTH),
                         lambda bb, i: (jnp.minimum((bb * nt + i + 1) * hb, last), 0)),
            pl.BlockSpec((32, C_WIDTH), lambda bb, i: (0, 0)),
            row, row, row,
        ],
        out_specs=pl.BlockSpec((CONV_T, C_WIDTH), lambda bb, i: (bb * nt + i, 0)),
        out_shape=jax.ShapeDtypeStruct((T, C_WIDTH), BF16),
        scratch_shapes=[pltpu.VMEM((CONV_T + 2 * CONV_HALO, C_WIDTH), F32)],
        compiler_params=_cparams(("parallel", "arbitrary")),
        name="conv_branch",
    )(zc, zc, zc, w, b, lg, lb)


def _diff_attn_kernel(slope_ref, q_ref, k_ref, vt_ref, lq1_ref, lk1_ref, lq2_ref, lk2_ref, sg_ref,
                      o_ref, qs_ref, ek_ref, dbias_ref, s_ref, mc_ref, m_ref, acc_ref, *, lam_init):
    h = pl.program_id(1)
    i = pl.program_id(2)
    tq = q_ref.shape[0]
    nkv = vt_ref.shape[0]
    assert tq == TK
    slope = slope_ref[h] * LOG2E

    @pl.when(i == 0)
    def _():
        lane = lax.broadcasted_iota(jnp.int32, (tq, LANES), 1)
        off = lax.broadcasted_iota(jnp.int32, (tq, LANES), 0).astype(F32) * slope
        hi = off.astype(BF16).astype(F32)
        mid = (off - hi).astype(BF16).astype(F32)
        lo = (off - hi - mid).astype(BF16).astype(F32)
        ek_ref[...] = jnp.where(lane == 0, hi, jnp.where(lane == 1, mid, jnp.where(lane == 2, lo,
                                jnp.where(lane < 6, 1.0, 0.0)))).astype(BF16)
        qx_before = jnp.where(lane < 3, 1.0, jnp.where(lane == 3, -hi, jnp.where(lane == 4, -mid,
                              jnp.where(lane == 5, -lo, 0.0))))
        for v, qx in enumerate((jnp.zeros_like(qx_before), qx_before, -qx_before)):
            qxt = qx.T.astype(BF16)
            qs_ref[v, LANES:, :tq] = qxt
            qs_ref[v, LANES:, tq:] = qxt
        rel = (lax.broadcasted_iota(jnp.int32, (TK, tq), 1) - lax.broadcasted_iota(jnp.int32, (TK, tq), 0))
        dbias_ref[...] = -slope * jnp.abs(rel).astype(F32)

    qt = q_ref[...].astype(F32).T
    qrow = lax.broadcasted_iota(jnp.int32, qt.shape, 0)
    q0t = jnp.where(qrow < HEAD_DIM, qt, 0.0).astype(BF16)
    q1t = jnp.where(qrow >= HEAD_DIM, qt, 0.0).astype(BF16)
    for v in range(3):
        qs_ref[v, :LANES, :tq] = q0t
        qs_ref[v, :LANES, tq:] = q1t

    def tile(t):
        return jnp.where(t == 0, i, t - 1 + (t - 1 >= i).astype(jnp.int32))

    def offset(t):
        return -slope * (jnp.abs(i - tile(t)) * TK).astype(F32)

    def scores(t, first):
        j = tile(t)
        kt = k_ref[pl.ds(pl.multiple_of(j * TK, TK), TK), :]
        kaug = jnp.concatenate([kt, ek_ref[...]], axis=1)
        if first:
            d = dbias_ref[...]
            return _dot(kaug, qs_ref[0]) + jnp.concatenate([d, d], axis=1)
        return _dot(kaug, qs_ref[jnp.where(j < i, 1, 2)])

    _flash_loop(nkv, scores, tile, offset, vt_ref, s_ref, mc_ref, m_ref, acc_ref)

    lam = (jnp.exp(jnp.sum(lq1_ref[...] * lk1_ref[...], axis=-1, keepdims=True))
           - jnp.exp(jnp.sum(lq2_ref[...] * lk2_ref[...], axis=-1, keepdims=True)) + lam_init)
    acc = acc_ref[...]
    o = acc[:D_V_DIFF] / acc[D_V_DIFF:D_V_DIFF + 1]
    od = (o[:, :tq] - lam * o[:, tq:]).T
    ms = jnp.mean(od * od, axis=-1, keepdims=True)
    od = od * lax.rsqrt(ms + EPS) * sg_ref[...] * (1.0 - lam_init)
    o_ref[...] = od.astype(o_ref.dtype)


def _diff_attn(zq, vt, lam_params, subln_g, B, L, lam_init):
    T = B * L
    tq = TQ_DIFF
    nq = L // tq
    nkv = L // TK
    nqs = 2 * tq
    dv = D_V_DIFF + VPAD
    slopes = jnp.asarray(ALIBI_SLOPES, F32)
    vec = pl.BlockSpec((1, HEAD_DIM), lambda b, h, i: (0, 0))
    return pl.pallas_call(
        functools.partial(_diff_attn_kernel, lam_init=lam_init),
        grid=(B, N_HEADS_D, nq),
        in_specs=[
            pl.BlockSpec(memory_space=pltpu.SMEM),
            pl.BlockSpec((tq, LANES), lambda b, h, i: (b * nq + i, h)),
            pl.BlockSpec((L, LANES), lambda b, h, i: (b, N_HEADS_D + h)),
            pl.BlockSpec((None, nkv, dv, TK), lambda b, h, i: (h, b, 0, 0)),
            vec, vec, vec, vec,
            pl.BlockSpec((1, D_V_DIFF), lambda b, h, i: (0, 0)),
        ],
        out_specs=pl.BlockSpec((tq, LANES), lambda b, h, i: (b * nq + i, h)),
        out_shape=jax.ShapeDtypeStruct((T, N_HEADS_D * D_V_DIFF), BF16),
        scratch_shapes=[pltpu.VMEM((3, 2 * LANES, nqs), BF16), pltpu.VMEM((TK, LANES), BF16),
                        pltpu.VMEM((TK, tq), F32),
                        pltpu.VMEM((2, TK, nqs), F32), pltpu.VMEM((2, 1, nqs), F32),
                        pltpu.VMEM((1, nqs), F32), pltpu.VMEM((dv, nqs), F32)],
        compiler_params=_cparams(("parallel", "parallel", "arbitrary")),
        name="diff_attn",
    )(slopes, zq, zq, vt, *lam_params, subln_g)


def _rope_tables(L):
    t = np.arange(L)
    row = (t // GRID_W).astype(np.float64)
    col = (t % GRID_W).astype(np.float64)
    half = HEAD_DIM // 2
    inv = ROPE_THETA ** (-np.arange(0, half, 2, dtype=np.float64) / half)
    ang = np.concatenate([row[:, None] * inv, col[:, None] * inv], axis=-1)
    cos = np.cos(ang).astype(np.float32)
    sin = np.sin(ang).astype(np.float32)
    cos_h = np.concatenate([cos, cos], axis=-1)
    sin_h = np.concatenate([-sin, sin], axis=-1)
    return (jnp.asarray(np.tile(cos_h, (1, LANES // HEAD_DIM))),
            jnp.asarray(np.tile(sin_h, (1, LANES // HEAD_DIM))))


_DEINT = np.concatenate([np.arange(0, HEAD_DIM, 2), np.arange(1, HEAD_DIM, 2)])


def _prep_weights(p):
    c = SCALE * LOG2E
    w = {}
    wa = N_HEADS_A * HEAD_DIM
    wq = N_HEADS_B * HEAD_DIM
    wk = N_KV_B * HEAD_DIM
    w_ab = p['w_in_ab']
    qcols = (np.arange(N_HEADS_B)[:, None] * HEAD_DIM + _DEINT[None, :]).reshape(-1)
    kcols = (np.arange(N_KV_B)[:, None] * HEAD_DIM + _DEINT[None, :]).reshape(-1)
    w['w_in_ab'] = jnp.concatenate([
        w_ab[:, :, :wa] * c,
        w_ab[:, :, wa:3 * wa],
        w_ab[:, :, 3 * wa:3 * wa + wq][:, :, qcols],
        w_ab[:, :, 3 * wa + wq:3 * wa + wq + wk][:, :, kcols],
        w_ab[:, :, 3 * wa + wq + wk:],
    ], axis=-1).astype(BF16)
    w['gq'] = jnp.tile(p['qnorm_b'][:, _DEINT] * c, (1, N_HEADS_B))[:, None, :]
    w['gk'] = jnp.tile(p['knorm_b'][:, _DEINT], (1, N_KV_B))[:, None, :]
    w_cd = p['w_in_cd']
    wd = N_HEADS_D * 2 * HEAD_DIM
    w['w_in_cd'] = jnp.concatenate([
        w_cd[:, :, :2 * C_WIDTH],
        w_cd[:, :, 2 * C_WIDTH:2 * C_WIDTH + wd] * c,
        w_cd[:, :, 2 * C_WIDTH + wd:],
    ], axis=-1).astype(BF16)
    for name in ('w_out_ab', 'w_out_cd', 'w1', 'w3', 'w2'):
        w[name] = p[name].astype(BF16)
    seg = np.kron(np.eye(N_HEADS_B, dtype=np.float32), np.ones((HEAD_DIM, HEAD_DIM), np.float32))
    w['seg'] = jnp.asarray(seg, BF16)
    w['na_bias'] = [_na_bias_table(p['rpb_a'][j]) for j in range(p['rpb_a'].shape[0])]
    w['conv_w'] = jnp.pad(p['conv_w_c'], ((0, 0), (0, 32 - CONV_K), (0, 0)))
    return w


def _trunk(x3, mod, p, w, rope):
    B, L, _ = x3.shape
    x = x3.reshape(B * L, D_MODEL)
    for li in range(DEPTH):
        sh1, sc1, g1, sh2, sc2, g2 = [mod[li, :, k] for k in range(6)]
        j = li // 2
        if li % 2 == 0:
            za, q, k, vt = _inproj_ab(x, p['norm_mix_g'][li], sc1, sh1, w['w_in_ab'][j], rope[0], rope[1],
                                      w['gq'][j], w['gk'][j], w['seg'], L)
            oa = _na_attn(za, w['na_bias'][j], B, L)
            mix, w_out = (oa, _gqa_attn(q, k, vt, B, L)), w['w_out_ab'][j]
        else:
            zc, zq, vt = _inproj(x, p['norm_mix_g'][li], sc1, sh1, w['w_in_cd'][j], L,
                                 (2 * C_WIDTH, 2 * N_HEADS_D * D_V_DIFF), (F32, BF16), vt_heads=N_HEADS_D)
            u = _conv_branch(zc, w['conv_w'][j], p['conv_b_c'][j][None], p['conv_ln_g'][j][None],
                             p['conv_ln_b'][j][None], B, L)
            lam_init = 0.8 - 0.6 * math.exp(-0.3 * li)
            lam_params = [p[n][j][None] for n in ('lam_q1', 'lam_k1', 'lam_q2', 'lam_k2')]
            od = _diff_attn(zq, vt, lam_params, p['subln_g'][j][None], B, L, lam_init)
            mix, w_out = (u, od), w['w_out_cd'][j]
        x = _out_ffn(x, mix[0], mix[1], g1, w_out, p['norm_ffn_g'][li], sc2, sh2, g2,
                     w['w1'][li], w['w3'][li], w['w2'][li], p['final_g'], L, final=(li == DEPTH - 1))
    return x.reshape(B, L, D_MODEL)


def kernel(x_prompt, x_sample, c_prompt, c_sample, w_mod, b_mod, norm_mix_g, norm_ffn_g, w_in_ab, rpb_a,
           qnorm_b, knorm_b, w_out_ab, w_in_cd, conv_w_c, conv_b_c, conv_ln_g, conv_ln_b, lam_q1, lam_k1,
           lam_q2, lam_k2, subln_g, w_out_cd, w1, w3, w2, final_g):
    p = dict(norm_mix_g=norm_mix_g, norm_ffn_g=norm_ffn_g, w_in_ab=w_in_ab, rpb_a=rpb_a, qnorm_b=qnorm_b,
             knorm_b=knorm_b, w_out_ab=w_out_ab, w_in_cd=w_in_cd, conv_w_c=conv_w_c, conv_b_c=conv_b_c,
             conv_ln_g=conv_ln_g, conv_ln_b=conv_ln_b, lam_q1=lam_q1, lam_k1=lam_k1, lam_q2=lam_q2,
             lam_k2=lam_k2, subln_g=subln_g, w_out_cd=w_out_cd, w1=w1, w3=w3, w2=w2, final_g=final_g)
    w = _prep_weights(p)
    bp, bs = c_prompt.shape[0], c_sample.shape[0]
    rows = -(-(bp + bs) // 8) * 8
    c_all = jnp.concatenate([c_prompt, c_sample, jnp.zeros((rows - bp - bs, D_MODEL), F32)], axis=0)
    mod = _modulation(c_all, w_mod, b_mod).reshape(DEPTH, rows, 6, 1, D_MODEL)
    y_prompt = _trunk(x_prompt, mod[:, :bp], p, w, _rope_tables(x_prompt.shape[1]))
    y_sample = _trunk(x_sample, mod[:, bp:bp + bs], p, w, _rope_tables(x_sample.shape[1]))
    return (y_prompt, y_sample)
```
